```python
import math
import jax, jax.numpy as jnp
from jax import lax
import numpy as np

D_MODEL = 2048
BATCH = 2
SEQ = 16384
DEPTH = 1

CONV_WIDTH = 1024
CONV_TAPS = 3
ATT_HEADS = 4
ATT_HEAD_DIM = 128
ATT_WIDTH = ATT_HEADS * 2 * ATT_HEAD_DIM
IN_WIDTH = 3 * CONV_WIDTH + 3 * ATT_WIDTH
IN_SPLITS = [CONV_WIDTH, 2 * CONV_WIDTH, 3 * CONV_WIDTH,
             3 * CONV_WIDTH + ATT_WIDTH, 3 * CONV_WIDTH + 2 * ATT_WIDTH]
N_BRANCHES = 2
D_FF = 5632
PLE_DIM = 256
N_BUCKETS = 32
MAX_DISTANCE = 128
Q_BLOCK = 128
EPS = 1e-6

kernel_name = 'hybrid_conv_diffattn_macaron_encoder'


def rms_norm(x, g):
    xf = x.astype(jnp.float32)
    y = xf * lax.rsqrt(jnp.mean(xf * xf, axis=-1, keepdims=True) + EPS)
    return (y * g.astype(jnp.float32)).astype(x.dtype)


def swiglu(x, w1, w3, w2):
    return (jax.nn.silu(x @ w1) * (x @ w3)) @ w2


def short_conv(z, w):
    zp = jnp.pad(z, ((0, 0), (1, 1), (0, 0)))
    return zp[:, :-2] * w[0] + zp[:, 1:-1] * w[1] + zp[:, 2:] * w[2]


def t5_bucket(rel):
    nb = N_BUCKETS // 2
    max_exact = nb // 2
    ret = jnp.where(rel > 0, nb, 0).astype(jnp.int32)
    n = jnp.abs(rel)
    nf = jnp.maximum(n, 1).astype(jnp.float32)
    large = max_exact + (jnp.log(nf / max_exact) / math.log(MAX_DISTANCE / max_exact)
                         * (nb - max_exact)).astype(jnp.int32)
    large = jnp.minimum(large, nb - 1)
    return ret + jnp.where(n < max_exact, n, large)


def diff_attention(q, k, v, lam, rel_bias):
    B, S = q.shape[0], q.shape[1]
    nblk = S // Q_BLOCK
    q_blocks = q.reshape(B, nblk, Q_BLOCK, ATT_HEADS, 2, ATT_HEAD_DIM).swapaxes(0, 1)
    k_pos = jnp.arange(S, dtype=jnp.int32)
    bias_table = rel_bias.T.astype(jnp.float32)
    scale = ATT_HEAD_DIM ** -0.5

    def block(args):
        q_blk, i = args
        q_pos = i * Q_BLOCK + jnp.arange(Q_BLOCK, dtype=jnp.int32)
        bias = bias_table[:, t5_bucket(k_pos[None, :] - q_pos[:, None])]
        logits = jnp.einsum('bqhcd,bkhcd->bchqk', q_blk, k).astype(jnp.float32) * scale + bias
        probs = jax.nn.softmax(logits, axis=-1)
        w = probs[:, 0] - lam.astype(jnp.float32) * probs[:, 1]
        return jnp.einsum('bhqk,bkhe->bqhe', w.astype(v.dtype), v)

    out = lax.map(block, (q_blocks, jnp.arange(nblk, dtype=jnp.int32)))
    return out.swapaxes(0, 1).reshape(B, S, ATT_HEADS, 2 * ATT_HEAD_DIM)


def setup_inputs(seed: int = 0) -> dict:
    key = jax.random.key(seed)
    ks = iter(jax.random.split(key, 32))

    def lin(fan_in, fan_out):
        return jax.random.normal(next(ks), (DEPTH, fan_in, fan_out), jnp.float32) * fan_in ** -0.5

    def gain(dim):
        return 1.0 + 0.02 * jax.random.normal(next(ks), (DEPTH, dim), jnp.float32)

    def small(shape, s):
        return s * jax.random.normal(next(ks), shape, jnp.float32)

    return {
        'x': jax.random.normal(next(ks), (BATCH, SEQ, D_MODEL), jnp.float32),
        'p': jax.random.normal(next(ks), (DEPTH, BATCH, SEQ, PLE_DIM), jnp.float32),
        'ffn1_norm': gain(D_MODEL),
        'ffn1_w1': lin(D_MODEL, D_FF),
        'ffn1_w3': lin(D_MODEL, D_FF),
        'ffn1_w2': lin(D_FF, D_MODEL),
        'mix_norm': gain(D_MODEL),
        'w_in': lin(D_MODEL, IN_WIDTH),
        'conv_w': small((DEPTH, CONV_TAPS, CONV_WIDTH), CONV_TAPS ** -0.5),
        'q_norm': gain(ATT_HEAD_DIM),
        'k_norm': gain(ATT_HEAD_DIM),
        'lam_q1': small((DEPTH, ATT_HEAD_DIM), 0.1),
        'lam_k1': small((DEPTH, ATT_HEAD_DIM), 0.1),
        'lam_q2': small((DEPTH, ATT_HEAD_DIM), 0.1),
        'lam_k2': small((DEPTH, ATT_HEAD_DIM), 0.1),
        'sub_norm': gain(2 * ATT_HEAD_DIM),
        'rel_bias': small((N_BUCKETS, ATT_HEADS), 0.5),
        'w_branch_a': lin(CONV_WIDTH, D_MODEL),
        'w_branch_b': lin(ATT_WIDTH, D_MODEL),
        'w_gate': lin(D_MODEL, N_BRANCHES * D_MODEL),
        'w_out': lin(D_MODEL, D_MODEL),
        'ffn2_norm': gain(D_MODEL),
        'ffn2_w1': lin(D_MODEL, D_FF),
        'ffn2_w3': lin(D_MODEL, D_FF),
        'ffn2_w2': lin(D_FF, D_MODEL),
        'ple_norm': gain(D_MODEL),
        'w_ple_gate': lin(D_MODEL, D_MODEL),
        'w_ple_proj': lin(PLE_DIM, D_MODEL),
    }


def reference(x, p, ffn1_norm, ffn1_w1, ffn1_w3, ffn1_w2, mix_norm, w_in, conv_w,
              q_norm, k_norm, lam_q1, lam_k1, lam_q2, lam_k2, sub_norm, rel_bias,
              w_branch_a, w_branch_b, w_gate, w_out, ffn2_norm, ffn2_w1, ffn2_w3, ffn2_w2,
              ple_norm, w_ple_gate, w_ple_proj):
    B, S = x.shape[0], x.shape[1]
    h = x
    for l in range(DEPTH):
        h = h + 0.5 * swiglu(rms_norm(h, ffn1_norm[l]), ffn1_w1[l], ffn1_w3[l], ffn1_w2[l])

        u = rms_norm(h, mix_norm[l])
        a_in, c_gate, b_gate, q, k, v = jnp.split(u @ w_in[l], IN_SPLITS, axis=-1)

        y_a = (b_gate * short_conv(c_gate * a_in, conv_w[l])) @ w_branch_a[l]

        q = rms_norm(q.reshape(B, S, ATT_HEADS, 2, ATT_HEAD_DIM), q_norm[l])
        k = rms_norm(k.reshape(B, S, ATT_HEADS, 2, ATT_HEAD_DIM), k_norm[l])
        v = v.reshape(B, S, ATT_HEADS, 2 * ATT_HEAD_DIM)
        lam_init = 0.8 - 0.6 * math.exp(-0.3 * l)
        lam = (jnp.exp(jnp.sum(lam_q1[l] * lam_k1[l])) - jnp.exp(jnp.sum(lam_q2[l] * lam_k2[l]))
               + lam_init)
        o = diff_attention(q, k, v, lam, rel_bias)
        o = rms_norm(o, sub_norm[l]) * (1.0 - lam_init)
        y_b = o.reshape(B, S, ATT_WIDTH) @ w_branch_b[l]

        g_a, g_b = jnp.split(jax.nn.sigmoid(u @ w_gate[l]), N_BRANCHES, axis=-1)
        h = h + (g_a * y_a + g_b * y_b) @ w_out[l]

        h = h + 0.5 * swiglu(rms_norm(h, ffn2_norm[l]), ffn2_w1[l], ffn2_w3[l], ffn2_w2[l])

        gate = jax.nn.sigmoid(rms_norm(h, ple_norm[l]) @ w_ple_gate[l])
        h = h + gate * (p[l] @ w_ple_proj[l])
    return h
```

```python
import functools
import math

import jax
import jax.numpy as jnp
from jax import lax
from jax.experimental import pallas as pl
from jax.experimental.pallas import tpu as pltpu

EPS = 1e-6
N_BUCKETS = 32
MAX_DISTANCE = 128
ATT_HEADS = 4
ATT_HEAD_DIM = 128
HEAD_W = 2 * ATT_HEAD_DIM
NEG_INIT = -1e30

VMEM_LIMIT_BYTES = 56 * 1024 * 1024

ROW_TILE = 512
FF_TILE = 512
ATT_TILE = 512
MERGE_COL_TILE = 512

_LOG_BUCKET_STARTS = (12, 16, 23, 32, 46, 64, 91)
FAR_DISTANCE = _LOG_BUCKET_STARTS[-1]


def _dot(a, b):
    return jnp.dot(a, b, preferred_element_type=jnp.float32)


def _dot_nt(a, b):
    return lax.dot_general(a, b, (((1,), (1,)), ((), ())),
                           preferred_element_type=jnp.float32)


def _rms(x, g):
    ms = jnp.mean(x * x, axis=-1, keepdims=True)
    return x * lax.rsqrt(ms + EPS) * g


def _params(*sem):
    return pltpu.CompilerParams(dimension_semantics=sem,
                                vmem_limit_bytes=VMEM_LIMIT_BYTES)


def _ffn_kernel(x_ref, g_ref, w1_ref, w3_ref, w2_ref, o_ref, xn_ref):
    j = pl.program_id(1)

    @pl.when(j == 0)
    def _():
        xn_ref[...] = _rms(x_ref[...], g_ref[...]).astype(jnp.bfloat16)
        o_ref[...] = jnp.zeros_like(o_ref)

    xn = xn_ref[...]
    gate = _dot(xn, w1_ref[...])
    up = _dot(xn, w3_ref[...])
    act = (gate * jax.nn.sigmoid(gate) * up).astype(jnp.bfloat16)
    o_ref[...] += _dot(act, w2_ref[...])

    @pl.when(j == pl.num_programs(1) - 1)
    def _():
        o_ref[...] = x_ref[...] + 0.5 * o_ref[...]


def _ffn(x, g, w1, w3, w2):
    n, d = x.shape
    f = w1.shape[1]
    tm, tf = ROW_TILE, FF_TILE
    return pl.pallas_call(
        _ffn_kernel,
        grid=(n // tm, f // tf),
        in_specs=[
            pl.BlockSpec((tm, d), lambda i, j: (i, 0)),
            pl.BlockSpec((1, d), lambda i, j: (0, 0)),
            pl.BlockSpec((d, tf), lambda i, j: (0, j)),
            pl.BlockSpec((d, tf), lambda i, j: (0, j)),
            pl.BlockSpec((tf, d), lambda i, j: (j, 0)),
        ],
        out_specs=pl.BlockSpec((tm, d), lambda i, j: (i, 0)),
        out_shape=jax.ShapeDtypeStruct((n, d), jnp.float32),
        scratch_shapes=[pltpu.VMEM((tm, d), jnp.bfloat16)],
        compiler_params=_params("parallel", "arbitrary"),
        name="ffn",
    )(x, g, w1, w3, w2)


def _proj_kernel(h_ref, g_ref, w_ref, qg_ref, kg_ref,
                 z_ref, b_ref, q_ref, k_ref, v_ref, u_ref):
    @pl.when(pl.program_id(1) == 0)
    def _():
        u_ref[...] = _rms(h_ref[...], g_ref[...]).astype(jnp.bfloat16)

    y = _dot(u_ref[...], w_ref[0])
    w = HEAD_W
    z_ref[...] = y[:, 0:w] * y[:, w:2 * w]
    b_ref[...] = y[:, 2 * w:3 * w]
    d = ATT_HEAD_DIM
    for c in range(2):
        lo = 3 * w + c * d
        q_ref[:, c * d:(c + 1) * d] = _rms(y[:, lo:lo + d], qg_ref[...]).astype(jnp.bfloat16)
        lo = 4 * w + c * d
        k_ref[:, c * d:(c + 1) * d] = _rms(y[:, lo:lo + d], kg_ref[...]).astype(jnp.bfloat16)
    v_ref[...] = y[:, 5 * w:6 * w].astype(jnp.bfloat16)


def _proj(h, g, w_cat, q_gain, k_gain):
    n, d = h.shape
    nblk = w_cat.shape[0]
    tm, w = ROW_TILE, HEAD_W
    col = pl.BlockSpec((tm, w), lambda i, j: (i, j))
    f32 = jax.ShapeDtypeStruct((n, nblk * w), jnp.float32)
    bf16 = jax.ShapeDtypeStruct((n, nblk * w), jnp.bfloat16)
    return pl.pallas_call(
        _proj_kernel,
        grid=(n // tm, nblk),
        in_specs=[
            pl.BlockSpec((tm, d), lambda i, j: (i, 0)),
            pl.BlockSpec((1, d), lambda i, j: (0, 0)),
            pl.BlockSpec((1, d, 6 * w), lambda i, j: (j, 0, 0)),
            pl.BlockSpec((1, ATT_HEAD_DIM), lambda i, j: (0, 0)),
            pl.BlockSpec((1, ATT_HEAD_DIM), lambda i, j: (0, 0)),
        ],
        out_specs=[col, col, col, col, col],
        out_shape=[f32, f32, bf16, bf16, bf16],
        scratch_shapes=[pltpu.VMEM((tm, d), jnp.bfloat16)],
        compiler_params=_params("parallel", "arbitrary"),
        name="proj",
    )(h, g, w_cat, q_gain, k_gain)


def _bias_tile_kernel(tab_ref, o_ref):
    h = pl.program_id(0)
    t = o_ref.shape[-1]
    offset = (pl.program_id(1) - 1) * t
    rel = (lax.broadcasted_iota(jnp.int32, (t, t), 1)
           - lax.broadcasted_iota(jnp.int32, (t, t), 0) + offset)
    n = jnp.abs(rel)
    nb = N_BUCKETS // 2
    max_exact = nb // 2
    large = jnp.full((t, t), max_exact, jnp.int32)
    for start in _LOG_BUCKET_STARTS:
        large = large + (n >= start).astype(jnp.int32)
    bucket = jnp.where(rel > 0, nb, 0) + jnp.where(n < max_exact, n, large)
    acc = jnp.zeros((t, t), jnp.float32)
    for bkt in range(N_BUCKETS):
        acc = jnp.where(bucket == bkt, tab_ref[bkt * ATT_HEADS + h], acc)
    o_ref[0, 0] = acc


def _bias_tiles(rel_bias_flat, t):
    return pl.pallas_call(
        _bias_tile_kernel,
        grid=(ATT_HEADS, 3),
        in_specs=[pl.BlockSpec(memory_space=pltpu.SMEM)],
        out_specs=pl.BlockSpec((1, 1, t, t), lambda h, o: (h, o, 0, 0)),
        out_shape=jax.ShapeDtypeStruct((ATT_HEADS, 3, t, t), jnp.float32),
        compiler_params=_params("parallel", "parallel"),
        name="bias_tiles",
    )(rel_bias_flat)


def _attn_kernel(tab_ref, lam_ref, q_ref, k_ref, v_ref, bias_ref, sg_ref, o_ref,
                 m_ref, l_ref, acc_ref, *, lam_init):
    h = pl.program_id(1)
    qi = pl.program_id(2)
    t = q_ref.shape[0]
    nk = k_ref.shape[0] // t
    d = ATT_HEAD_DIM
    scale = d ** -0.5

    m_ref[...] = jnp.full_like(m_ref, NEG_INIT)
    l_ref[...] = jnp.zeros_like(l_ref)
    acc_ref[...] = jnp.zeros_like(acc_ref)

    def chunk(kc, bias):
        rows = pl.ds(pl.multiple_of(kc * t, t), t)
        v = v_ref[rows, :]
        for c in range(2):
            q = q_ref[:, c * d:(c + 1) * d]
            k = k_ref[rows, c * d:(c + 1) * d]
            x = _dot_nt(q, k) * scale + bias
            m_old = m_ref[c]
            m_new = jnp.maximum(m_old, jnp.max(x, axis=-1, keepdims=True))
            alpha = jnp.exp(m_old - m_new)
            p = jnp.exp(x - m_new)
            l_ref[c] = alpha * l_ref[c] + jnp.sum(p, axis=-1, keepdims=True)
            acc_ref[c] = alpha * acc_ref[c] + _dot(p.astype(jnp.bfloat16), v)
            m_ref[c] = m_new

    nb = N_BUCKETS // 2
    bias_left = tab_ref[(nb - 1) * ATT_HEADS + h]
    bias_right = tab_ref[(N_BUCKETS - 1) * ATT_HEADS + h]

    def left_body(kc, carry):
        chunk(kc, bias_left)
        return carry

    def right_body(kc, carry):
        chunk(kc, bias_right)
        return carry

    lax.fori_loop(0, qi - 1, left_body, 0)
    for o in range(3):
        kc = qi + (o - 1)

        @pl.when(jnp.logical_and(kc >= 0, kc < nk))
        def _():
            chunk(kc, bias_ref[0, o])

    lax.fori_loop(qi + 2, nk, right_body, 0)

    lam_q1, lam_k1, lam_q2, lam_k2 = (lam_ref[r:r + 1, :] for r in range(4))
    lam = (jnp.exp(jnp.sum(lam_q1 * lam_k1, axis=-1, keepdims=True))
           - jnp.exp(jnp.sum(lam_q2 * lam_k2, axis=-1, keepdims=True)) + lam_init)
    o = acc_ref[0] / l_ref[0] - lam * (acc_ref[1] / l_ref[1])
    o_ref[...] = (_rms(o, sg_ref[...]) * (1.0 - lam_init)).astype(jnp.bfloat16)


def _attn(tab_flat, lam_rows, q, k, v, bias_tiles, sub_gain, batch, seq, lam_init):
    n, width = q.shape
    t = ATT_TILE
    assert FAR_DISTANCE <= t and seq % t == 0
    nq = seq // t
    w = HEAD_W
    return pl.pallas_call(
        functools.partial(_attn_kernel, lam_init=lam_init),
        grid=(batch, ATT_HEADS, nq),
        in_specs=[
            pl.BlockSpec(memory_space=pltpu.SMEM),
            pl.BlockSpec((4, ATT_HEAD_DIM), lambda b, h, i: (0, 0)),
            pl.BlockSpec((t, w), lambda b, h, i: (b * nq + i, h)),
            pl.BlockSpec((seq, w), lambda b, h, i: (b, h)),
            pl.BlockSpec((seq, w), lambda b, h, i: (b, h)),
            pl.BlockSpec((1, 3, t, t), lambda b, h, i: (h, 0, 0, 0)),
            pl.BlockSpec((1, w), lambda b, h, i: (0, 0)),
        ],
        out_specs=pl.BlockSpec((t, w), lambda b, h, i: (b * nq + i, h)),
        out_shape=jax.ShapeDtypeStruct((n, width), jnp.bfloat16),
        scratch_shapes=[
            pltpu.VMEM((2, t, 1), jnp.float32),
            pltpu.VMEM((2, t, 1), jnp.float32),
            pltpu.VMEM((2, t, w), jnp.float32),
        ],
        compiler_params=_params("parallel", "parallel", "arbitrary"),
        name="attn",
    )(tab_flat, lam_rows, q, k, v, bias_tiles, sub_gain)


def _merge_kernel(h_ref, g_ref, z_ref, zp_ref, zn_ref, b_ref, o_ref, cw_ref,
                  wg_ref, wa_ref, wb_ref, wo_ref, out_ref,
                  u_ref, a_ref, zs_ref, *, tiles_per_seq):
    i = pl.program_id(0)
    j = pl.program_id(1)
    tm = h_ref.shape[0]
    halo = zp_ref.shape[0]

    @pl.when(j == 0)
    def _():
        u_ref[...] = _rms(h_ref[...], g_ref[...]).astype(jnp.bfloat16)
        pos = i % tiles_per_seq
        zs_ref[0:halo, :] = jnp.where(pos == 0, 0.0, zp_ref[...])
        zs_ref[halo:halo + tm, :] = z_ref[...]
        zs_ref[halo + tm:2 * halo + tm, :] = jnp.where(pos == tiles_per_seq - 1, 0.0, zn_ref[...])
        conv = (zs_ref[halo - 1:halo - 1 + tm, :] * cw_ref[0:1, :]
                + zs_ref[halo:halo + tm, :] * cw_ref[1:2, :]
                + zs_ref[halo + 1:halo + 1 + tm, :] * cw_ref[2:3, :])
        a_ref[...] = (b_ref[...] * conv).astype(jnp.bfloat16)
        out_ref[...] = h_ref[...]

    tn = wa_ref.shape[1]
    gates = jax.nn.sigmoid(_dot(u_ref[...], wg_ref[0]))
    y_a = _dot(a_ref[...], wa_ref[...])
    y_b = _dot(o_ref[...], wb_ref[...])
    mixed = (gates[:, :tn] * y_a + gates[:, tn:] * y_b).astype(jnp.bfloat16)
    out_ref[...] += _dot(mixed, wo_ref[...])


def _merge(h, g, z, b, o, conv_w, wg_cat, wa, wb, wo, seq):
    n, d = h.shape
    cw = z.shape[1]
    tm, tn = ROW_TILE, MERGE_COL_TILE
    halo = 8
    nj = d // tn
    hb = tm // halo
    last_hb = n // halo - 1
    return pl.pallas_call(
        functools.partial(_merge_kernel, tiles_per_seq=seq // tm),
        grid=(n // tm, nj),
        in_specs=[
            pl.BlockSpec((tm, d), lambda i, j: (i, 0)),
            pl.BlockSpec((1, d), lambda i, j: (0, 0)),
            pl.BlockSpec((tm, cw), lambda i, j: (i, 0)),
            pl.BlockSpec((halo, cw), lambda i, j: (jnp.maximum(i * hb - 1, 0), 0)),
            pl.BlockSpec((halo, cw), lambda i, j: (jnp.minimum((i + 1) * hb, last_hb), 0)),
            pl.BlockSpec((tm, cw), lambda i, j: (i, 0)),
            pl.BlockSpec((tm, o.shape[1]), lambda i, j: (i, 0)),
            pl.BlockSpec((3, cw), lambda i, j: (0, 0)),
            pl.BlockSpec((1, d, 2 * tn), lambda i, j: (j, 0, 0)),
            pl.BlockSpec((cw, tn), lambda i, j: (0, j)),
            pl.BlockSpec((o.shape[1], tn), lambda i, j: (0, j)),
            pl.BlockSpec((tn, d), lambda i, j: (j, 0)),
        ],
        out_specs=pl.BlockSpec((tm, d), lambda i, j: (i, 0)),
        out_shape=jax.ShapeDtypeStruct((n, d), jnp.float32),
        scratch_shapes=[
            pltpu.VMEM((tm, d), jnp.bfloat16),
            pltpu.VMEM((tm, cw), jnp.bfloat16),
            pltpu.VMEM((tm + 2 * halo, cw), jnp.float32),
        ],
        compiler_params=_params("parallel", "arbitrary"),
        name="merge",
    )(h, g, z, z, z, b, o, conv_w, wg_cat, wa, wb, wo)


def _ple_kernel(h_ref, g_ref, p_ref, wg_ref, wp_ref, o_ref):
    h = h_ref[...]
    gate = jax.nn.sigmoid(_dot(_rms(h, g_ref[...]).astype(jnp.bfloat16), wg_ref[...]))
    o_ref[...] = h + gate * _dot(p_ref[...].astype(jnp.bfloat16), wp_ref[...])


def _ple(h, g, p, wg, wp):
    n, d = h.shape
    e = p.shape[1]
    tm = ROW_TILE
    return pl.pallas_call(
        _ple_kernel,
        grid=(n // tm,),
        in_specs=[
            pl.BlockSpec((tm, d), lambda i: (i, 0)),
            pl.BlockSpec((1, d), lambda i: (0, 0)),
            pl.BlockSpec((tm, e), lambda i: (i, 0)),
            pl.BlockSpec((d, d), lambda i: (0, 0)),
            pl.BlockSpec((e, d), lambda i: (0, 0)),
        ],
        out_specs=pl.BlockSpec((tm, d), lambda i: (i, 0)),
        out_shape=jax.ShapeDtypeStruct((n, d), jnp.float32),
        compiler_params=_params("parallel"),
        name="ple",
    )(h, g, p, wg, wp)


def _bf16(w):
    return w.astype(jnp.bfloat16)


def _col_blocks(w, width):
    d, c = w.shape
    return w.reshape(d, c // width, width).transpose(1, 0, 2)


def kernel(x, p, ffn1_norm, ffn1_w1, ffn1_w3, ffn1_w2, mix_norm, w_in, conv_w, q_norm, k_norm,
           lam_q1, lam_k1, lam_q2, lam_k2, sub_norm, rel_bias, w_branch_a, w_branch_b, w_gate,
           w_out, ffn2_norm, ffn2_w1, ffn2_w3, ffn2_w2, ple_norm, w_ple_gate, w_ple_proj):
    batch, seq, d = x.shape
    n = batch * seq
    depth = ffn1_w1.shape[0]
    h = x.reshape(n, d)

    tab_flat = rel_bias.astype(jnp.float32).reshape(-1)
    bias_tiles = _bias_tiles(tab_flat, ATT_TILE)

    for l in range(depth):
        row = lambda a: a[l].reshape(1, -1)
        h = _ffn(h, row(ffn1_norm), _bf16(ffn1_w1[l]), _bf16(ffn1_w3[l]), _bf16(ffn1_w2[l]))

        w_cat = jnp.concatenate(
            [_col_blocks(part, HEAD_W) for part in jnp.split(_bf16(w_in[l]), 6, axis=1)], axis=-1)
        z, b_gate, q, k, v = _proj(h, row(mix_norm), w_cat, row(q_norm), row(k_norm))

        lam_init = 0.8 - 0.6 * math.exp(-0.3 * l)
        lam_rows = jnp.stack([lam_q1[l], lam_k1[l], lam_q2[l], lam_k2[l]])
        o = _attn(tab_flat, lam_rows, q, k, v, bias_tiles, row(sub_norm), batch, seq, lam_init)

        wg = _bf16(w_gate[l])
        wg_cat = jnp.concatenate([_col_blocks(wg[:, :d], MERGE_COL_TILE),
                                  _col_blocks(wg[:, d:], MERGE_COL_TILE)], axis=-1)
        h = _merge(h, row(mix_norm), z, b_gate, o, conv_w[l], wg_cat,
                   _bf16(w_branch_a[l]), _bf16(w_branch_b[l]), _bf16(w_out[l]), seq)

        h = _ffn(h, row(ffn2_norm), _bf16(ffn2_w1[l]), _bf16(ffn2_w3[l]), _bf16(ffn2_w2[l]))
        h = _ple(h, row(ple_norm), p[l].reshape(n, -1), _bf16(w_ple_gate[l]), _bf16(w_ple_proj[l]))
    return h.reshape(batch, seq, d)
```

```python
import functools
import math

import jax
import jax.numpy as jnp
from jax import lax
from jax.experimental import pallas as pl
from jax.experimental.pallas import tpu as pltpu

EPS = 1e-6
N_BUCKETS = 32
MAX_DISTANCE = 128
ATT_HEADS = 4
ATT_HEAD_DIM = 128
HEAD_W = 2 * ATT_HEAD_DIM
LANES = 128
NEG_INIT = -1e30
LOG2E = 1.0 / math.log(2.0)
Q_SCALE = LOG2E * ATT_HEAD_DIM ** -0.5

FIXED_SHIFT_MAX_SPAN = 100.0
BF16_ROUNDING_MARGIN = 1.01

VMEM_LIMIT_BYTES = 56 * 1024 * 1024

ROW_TILE = 512
FF_TILE = 512
ATT_TILE = 512
ATT_GROUP = 8
ATT_FAR_KEYS = 2048
BIAS_TILE_REACH = 2
MERGE_COL_TILE = 512

_LOG_BUCKET_STARTS = (12, 16, 23, 32, 46, 64, 91)
FAR_DISTANCE = _LOG_BUCKET_STARTS[-1]


def _dot(a, b):
    return jnp.dot(a, b, preferred_element_type=jnp.float32)


def _dot_nt(a, b):
    return lax.dot_general(a, b, (((1,), (1,)), ((), ())),
                           preferred_element_type=jnp.float32)


def _rms(x, g):
    ms = jnp.mean(x * x, axis=-1, keepdims=True)
    return x * lax.rsqrt(ms + EPS) * g


def _params(*sem):
    return pltpu.CompilerParams(dimension_semantics=sem,
                                vmem_limit_bytes=VMEM_LIMIT_BYTES)


def _ffn_kernel(x_ref, g_ref, w1_ref, w3_ref, w2_ref, o_ref, xn_ref):
    j = pl.program_id(1)

    @pl.when(j == 0)
    def _():
        xn_ref[...] = _rms(x_ref[...], g_ref[...]).astype(jnp.bfloat16)
        o_ref[...] = jnp.zeros_like(o_ref)

    xn = xn_ref[...]
    gate = _dot(xn, w1_ref[...])
    up = _dot(xn, w3_ref[...])
    act = (gate * jax.nn.sigmoid(gate) * up).astype(jnp.bfloat16)
    o_ref[...] += _dot(act, w2_ref[...])

    @pl.when(j == pl.num_programs(1) - 1)
    def _():
        o_ref[...] = x_ref[...] + 0.5 * o_ref[...]


def _ffn(x, g, w1, w3, w2):
    n, d = x.shape
    f = w1.shape[1]
    tm, tf = ROW_TILE, FF_TILE
    return pl.pallas_call(
        _ffn_kernel,
        grid=(n // tm, f // tf),
        in_specs=[
            pl.BlockSpec((tm, d), lambda i, j: (i, 0)),
            pl.BlockSpec((1, d), lambda i, j: (0, 0)),
            pl.BlockSpec((d, tf), lambda i, j: (0, j)),
            pl.BlockSpec((d, tf), lambda i, j: (0, j)),
            pl.BlockSpec((tf, d), lambda i, j: (j, 0)),
        ],
        out_specs=pl.BlockSpec((tm, d), lambda i, j: (i, 0)),
        out_shape=jax.ShapeDtypeStruct((n, d), jnp.float32),
        scratch_shapes=[pltpu.VMEM((tm, d), jnp.bfloat16)],
        compiler_params=_params("parallel", "arbitrary"),
        name="ffn",
    )(x, g, w1, w3, w2)


def _proj_kernel(h_ref, g_ref, w_ref, qg_ref, kg_ref,
                 z_ref, b_ref, q_ref, k_ref, v_ref, u_ref):
    @pl.when(pl.program_id(1) == 0)
    def _():
        u_ref[...] = _rms(h_ref[...], g_ref[...]).astype(jnp.bfloat16)

    y = _dot(u_ref[...], w_ref[0])
    w = HEAD_W
    z_ref[...] = y[:, 0:w] * y[:, w:2 * w]
    b_ref[...] = y[:, 2 * w:3 * w]
    d = ATT_HEAD_DIM
    for c in range(2):
        lo = 3 * w + c * d
        q = _rms(y[:, lo:lo + d], qg_ref[...]) * Q_SCALE
        q_ref[:, c * d:(c + 1) * d] = q.astype(jnp.bfloat16)
        lo = 4 * w + c * d
        k_ref[:, c * d:(c + 1) * d] = _rms(y[:, lo:lo + d], kg_ref[...]).astype(jnp.bfloat16)
    v_ref[...] = y[:, 5 * w:6 * w].astype(jnp.bfloat16)


def _proj(h, g, w_cat, q_gain, k_gain):
    n, d = h.shape
    nblk = w_cat.shape[0]
    tm, w = ROW_TILE, HEAD_W
    col = pl.BlockSpec((tm, w), lambda i, j: (i, j))
    f32 = jax.ShapeDtypeStruct((n, nblk * w), jnp.float32)
    bf16 = jax.ShapeDtypeStruct((n, nblk * w), jnp.bfloat16)
    return pl.pallas_call(
        _proj_kernel,
        grid=(n // tm, nblk),
        in_specs=[
            pl.BlockSpec((tm, d), lambda i, j: (i, 0)),
            pl.BlockSpec((1, d), lambda i, j: (0, 0)),
            pl.BlockSpec((1, d, 6 * w), lambda i, j: (j, 0, 0)),
            pl.BlockSpec((1, ATT_HEAD_DIM), lambda i, j: (0, 0)),
            pl.BlockSpec((1, ATT_HEAD_DIM), lambda i, j: (0, 0)),
        ],
        out_specs=[col, col, col, col, col],
        out_shape=[f32, f32, bf16, bf16, bf16],
        scratch_shapes=[pltpu.VMEM((tm, d), jnp.bfloat16)],
        compiler_params=_params("parallel", "arbitrary"),
        name="proj",
    )(h, g, w_cat, q_gain, k_gain)


def _bias_tile_kernel(tab_ref, o_ref):
    h = pl.program_id(0)
    t = o_ref.shape[-1]
    offset = (pl.program_id(1) - BIAS_TILE_REACH) * t
    rel = (lax.broadcasted_iota(jnp.int32, (t, t), 1)
           - lax.broadcasted_iota(jnp.int32, (t, t), 0) + offset)
    n = jnp.abs(rel)
    nb = N_BUCKETS // 2
    max_exact = nb // 2
    large = jnp.full((t, t), max_exact, jnp.int32)
    for start in _LOG_BUCKET_STARTS:
        large = large + (n >= start).astype(jnp.int32)
    bucket = jnp.where(rel > 0, nb, 0) + jnp.where(n < max_exact, n, large)
    acc = jnp.zeros((t, t), jnp.float32)
    for bkt in range(N_BUCKETS):
        acc = jnp.where(bucket == bkt, tab_ref[bkt * ATT_HEADS + h], acc)
    o_ref[0, 0] = acc


def _bias_tiles(table_flat, t):
    n_off = 2 * BIAS_TILE_REACH + 1
    return pl.pallas_call(
        _bias_tile_kernel,
        grid=(ATT_HEADS, n_off),
        in_specs=[pl.BlockSpec(memory_space=pltpu.SMEM)],
        out_specs=pl.BlockSpec((1, 1, t, t), lambda h, o: (h, o, 0, 0)),
        out_shape=jax.ShapeDtypeStruct((ATT_HEADS, n_off, t, t), jnp.float32),
        compiler_params=_params("parallel", "parallel"),
        name="bias_tiles",
    )(table_flat)


def _attn_kernel(tab_ref, fixed_ref, lam_ref, q_ref, k_ref, v_ref, bias_ref, sg_ref, o_ref,
                 m_ref, l_ref, lp_ref, acc_ref, *, lam_init):
    h = pl.program_id(1)
    qi = pl.program_id(2)
    t = q_ref.shape[0]
    d = ATT_HEAD_DIM

    nb = N_BUCKETS // 2
    bias_left = tab_ref[(nb - 1) * ATT_HEADS + h]
    bias_right = tab_ref[(N_BUCKETS - 1) * ATT_HEADS + h]

    def logits(start, size, c):
        rows = pl.ds(pl.multiple_of(start, t), size)
        return _dot_nt(q_ref[:, c * d:(c + 1) * d], k_ref[rows, c * d:(c + 1) * d])

    def values(start, size):
        return v_ref[pl.ds(pl.multiple_of(start, t), size), :]

    def fixed_step(start, size, bias):
        v = values(start, size)
        for c in range(2):
            p = jnp.exp2(logits(start, size, c) + bias)
            part = p[:, 0:LANES]
            for s in range(1, size // LANES):
                part = part + p[:, s * LANES:(s + 1) * LANES]
            lp_ref[c] += part
            acc_ref[c] += _dot(p.astype(jnp.bfloat16), v)

    def online_step(start, size, bias):
        v = values(start, size)
        for c in range(2):
            x = logits(start, size, c) + bias
            m_old = m_ref[c]
            m_new = jnp.maximum(m_old, jnp.max(x, axis=-1, keepdims=True))
            alpha = jnp.exp2(m_old - m_new)
            p = jnp.exp2(x - m_new)
            l_ref[c] = alpha * l_ref[c] + jnp.sum(p, axis=-1, keepdims=True)
            acc_ref[c] = alpha * acc_ref[c] + _dot(p.astype(jnp.bfloat16), v)
            m_ref[c] = m_new

    def sweep(step, group, far_size):
        def group_body(g, carry):
            first = g * group
            near = jnp.logical_and(qi >= first - 1, qi <= first + group)

            @pl.when(near)
            def _():
                for s in range(group):
                    off = jnp.clip(first + s - qi, -BIAS_TILE_REACH, BIAS_TILE_REACH)
                    step((first + s) * t, t, bias_ref[0, off + BIAS_TILE_REACH])

            @pl.when(jnp.logical_not(near))
            def _():
                bias = jnp.where(first < qi, bias_left, bias_right)
                for s in range(group * t // far_size):
                    step(first * t + s * far_size, far_size, bias)

            return carry

        lax.fori_loop(0, k_ref.shape[0] // (group * t), group_body, 0)

    acc_ref[...] = jnp.zeros_like(acc_ref)

    @pl.when(fixed_ref[0] == 1)
    def _():
        lp_ref[...] = jnp.zeros_like(lp_ref)
        sweep(fixed_step, ATT_GROUP, ATT_FAR_KEYS)
        for c in range(2):
            l_ref[c] = jnp.sum(lp_ref[c], axis=-1, keepdims=True)

    @pl.when(fixed_ref[0] != 1)
    def _():
        m_ref[...] = jnp.full_like(m_ref, NEG_INIT)
        l_ref[...] = jnp.zeros_like(l_ref)
        sweep(online_step, 1, t)

    lam_q1, lam_k1, lam_q2, lam_k2 = (lam_ref[r:r + 1, :] for r in range(4))
    lam = (jnp.exp(jnp.sum(lam_q1 * lam_k1, axis=-1, keepdims=True))
           - jnp.exp(jnp.sum(lam_q2 * lam_k2, axis=-1, keepdims=True)) + lam_init)
    o = acc_ref[0] / l_ref[0] - lam * (acc_ref[1] / l_ref[1])
    o_ref[...] = (_rms(o, sg_ref[...]) * (1.0 - lam_init)).astype(jnp.bfloat16)


def _attn(table_flat, fixed_flag, lam_rows, q, k, v, bias_tiles, sub_gain, batch, seq, lam_init):
    n, width = q.shape
    t = ATT_TILE
    assert FAR_DISTANCE <= (BIAS_TILE_REACH - 1) * t + 1
    assert seq % (ATT_GROUP * t) == 0 and (ATT_GROUP * t) % ATT_FAR_KEYS == 0
    nq = seq // t
    w = HEAD_W
    return pl.pallas_call(
        functools.partial(_attn_kernel, lam_init=lam_init),
        grid=(batch, ATT_HEADS, nq),
        in_specs=[
            pl.BlockSpec(memory_space=pltpu.SMEM),
            pl.BlockSpec(memory_space=pltpu.SMEM),
            pl.BlockSpec((4, ATT_HEAD_DIM), lambda b, h, i: (0, 0)),
            pl.BlockSpec((t, w), lambda b, h, i: (b * nq + i, h)),
            pl.BlockSpec((seq, w), lambda b, h, i: (b, h)),
            pl.BlockSpec((seq, w), lambda b, h, i: (b, h)),
            pl.BlockSpec((1, 2 * BIAS_TILE_REACH + 1, t, t), lambda b, h, i: (h, 0, 0, 0)),
            pl.BlockSpec((1, w), lambda b, h, i: (0, 0)),
        ],
        out_specs=pl.BlockSpec((t, w), lambda b, h, i: (b * nq + i, h)),
        out_shape=jax.ShapeDtypeStruct((n, width), jnp.bfloat16),
        scratch_shapes=[
            pltpu.VMEM((2, t, 1), jnp.float32),
            pltpu.VMEM((2, t, 1), jnp.float32),
            pltpu.VMEM((2, t, LANES), jnp.float32),
            pltpu.VMEM((2, t, w), jnp.float32),
        ],
        compiler_params=_params("parallel", "parallel", "arbitrary"),
        name="attn",
    )(table_flat, fixed_flag, lam_rows, q, k, v, bias_tiles, sub_gain)


def _shifted_bias_table(rel_bias, q_gain, k_gain):
    table = rel_bias.astype(jnp.float32) * LOG2E
    dot_bound = (Q_SCALE * ATT_HEAD_DIM * BF16_ROUNDING_MARGIN
                 * jnp.max(jnp.abs(q_gain)) * jnp.max(jnp.abs(k_gain)))
    span = 2.0 * dot_bound + (jnp.max(table) - jnp.min(table))
    fixed = (span <= FIXED_SHIFT_MAX_SPAN).astype(jnp.int32).reshape(1)
    shift = jnp.where(fixed[0] == 1, dot_bound + jnp.max(table), 0.0)
    return (table - shift).reshape(-1), fixed


def _merge_kernel(h_ref, g_ref, z_ref, zp_ref, zn_ref, b_ref, o_ref, cw_ref,
                  wg_ref, wa_ref, wb_ref, wo_ref, out_ref,
                  u_ref, a_ref, zs_ref, *, tiles_per_seq):
    i = pl.program_id(0)
    j = pl.program_id(1)
    tm = h_ref.shape[0]
    halo = zp_ref.shape[0]

    @pl.when(j == 0)
    def _():
        u_ref[...] = _rms(h_ref[...], g_ref[...]).astype(jnp.bfloat16)
        pos = i % tiles_per_seq
        zs_ref[0:halo, :] = jnp.where(pos == 0, 0.0, zp_ref[...])
        zs_ref[halo:halo + tm, :] = z_ref[...]
        zs_ref[halo + tm:2 * halo + tm, :] = jnp.where(pos == tiles_per_seq - 1, 0.0, zn_ref[...])
        conv = (zs_ref[halo - 1:halo - 1 + tm, :] * cw_ref[0:1, :]
                + zs_ref[halo:halo + tm, :] * cw_ref[1:2, :]
                + zs_ref[halo + 1:halo + 1 + tm, :] * cw_ref[2:3, :])
        a_ref[...] = (b_ref[...] * conv).astype(jnp.bfloat16)
        out_ref[...] = h_ref[...]

    tn = wa_ref.shape[1]
    gates = jax.nn.sigmoid(_dot(u_ref[...], wg_ref[0]))
    y_a = _dot(a_ref[...], wa_ref[...])
    y_b = _dot(o_ref[...], wb_ref[...])
    mixed = (gates[:, :tn] * y_a + gates[:, tn:] * y_b).astype(jnp.bfloat16)
    out_ref[...] += _dot(mixed, wo_ref[...])


def _merge(h, g, z, b, o, conv_w, wg_cat, wa, wb, wo, seq):
    n, d = h.shape
    cw = z.shape[1]
    tm, tn = ROW_TILE, MERGE_COL_TILE
    halo = 8
    nj = d // tn
    hb = tm // halo
    last_hb = n // halo - 1
    return pl.pallas_call(
        functools.partial(_merge_kernel, tiles_per_seq=seq // tm),
        grid=(n // tm, nj),
        in_specs=[
            pl.BlockSpec((tm, d), lambda i, j: (i, 0)),
            pl.BlockSpec((1, d), lambda i, j: (0, 0)),
            pl.BlockSpec((tm, cw), lambda i, j: (i, 0)),
            pl.BlockSpec((halo, cw), lambda i, j: (jnp.maximum(i * hb - 1, 0), 0)),
            pl.BlockSpec((halo, cw), lambda i, j: (jnp.minimum((i + 1) * hb, last_hb), 0)),
            pl.BlockSpec((tm, cw), lambda i, j: (i, 0)),
            pl.BlockSpec((tm, o.shape[1]), lambda i, j: (i, 0)),
            pl.BlockSpec((3, cw), lambda i, j: (0, 0)),
            pl.BlockSpec((1, d, 2 * tn), lambda i, j: (j, 0, 0)),
            pl.BlockSpec((cw, tn), lambda i, j: (0, j)),
            pl.BlockSpec((o.shape[1], tn), lambda i, j: (0, j)),
            pl.BlockSpec((tn, d), lambda i, j: (j, 0)),
        ],
        out_specs=pl.BlockSpec((tm, d), lambda i, j: (i, 0)),
        out_shape=jax.ShapeDtypeStruct((n, d), jnp.float32),
        scratch_shapes=[
            pltpu.VMEM((tm, d), jnp.bfloat16),
            pltpu.VMEM((tm, cw), jnp.bfloat16),
            pltpu.VMEM((tm + 2 * halo, cw), jnp.float32),
        ],
        compiler_params=_params("parallel", "arbitrary"),
        name="merge",
    )(h, g, z, z, z, b, o, conv_w, wg_cat, wa, wb, wo)


def _ple_kernel(h_ref, g_ref, p_ref, wg_ref, wp_ref, o_ref):
    h = h_ref[...]
    gate = jax.nn.sigmoid(_dot(_rms(h, g_ref[...]).astype(jnp.bfloat16), wg_ref[...]))
    o_ref[...] = h + gate * _dot(p_ref[...].astype(jnp.bfloat16), wp_ref[...])


def _ple(h, g, p, wg, wp):
    n, d = h.shape
    e = p.shape[1]
    tm = ROW_TILE
    return pl.pallas_call(
        _ple_kernel,
        grid=(n // tm,),
        in_specs=[
            pl.BlockSpec((tm, d), lambda i: (i, 0)),
            pl.BlockSpec((1, d), lambda i: (0, 0)),
            pl.BlockSpec((tm, e), lambda i: (i, 0)),
            pl.BlockSpec((d, d), lambda i: (0, 0)),
            pl.BlockSpec((e, d), lambda i: (0, 0)),
        ],
        out_specs=pl.BlockSpec((tm, d), lambda i: (i, 0)),
        out_shape=jax.ShapeDtypeStruct((n, d), jnp.float32),
        compiler_params=_params("parallel"),
        name="ple",
    )(h, g, p, wg, wp)


def _bf16(w):
    return w.astype(jnp.bfloat16)


def _col_blocks(w, width):
    d, c = w.shape
    return w.reshape(d, c // width, width).transpose(1, 0, 2)


def kernel(x, p, ffn1_norm, ffn1_w1, ffn1_w3, ffn1_w2, mix_norm, w_in, conv_w, q_norm, k_norm,
           lam_q1, lam_k1, lam_q2, lam_k2, sub_norm, rel_bias, w_branch_a, w_branch_b, w_gate,
           w_out, ffn2_norm, ffn2_w1, ffn2_w3, ffn2_w2, ple_norm, w_ple_gate, w_ple_proj):
    batch, seq, d = x.shape
    n = batch * seq
    depth = ffn1_w1.shape[0]
    h = x.reshape(n, d)

    for l in range(depth):
        row = lambda a: a[l].reshape(1, -1)
        h = _ffn(h, row(ffn1_norm), _bf16(ffn1_w1[l]), _bf16(ffn1_w3[l]), _bf16(ffn1_w2[l]))

        w_cat = jnp.concatenate(
            [_col_blocks(part, HEAD_W) for part in jnp.split(_bf16(w_in[l]), 6, axis=1)], axis=-1)
        z, b_gate, q, k, v = _proj(h, row(mix_norm), w_cat, row(q_norm), row(k_norm))

        lam_init = 0.8 - 0.6 * math.exp(-0.3 * l)
        lam_rows = jnp.stack([lam_q1[l], lam_k1[l], lam_q2[l], lam_k2[l]])
        table_flat, fixed_flag = _shifted_bias_table(rel_bias, q_norm[l], k_norm[l])
        bias_tiles = _bias_tiles(table_flat, ATT_TILE)
        o = _attn(table_flat, fixed_flag, lam_rows, q, k, v, bias_tiles, row(sub_norm),
                  batch, seq, lam_init)

        wg = _bf16(w_gate[l])
        wg_cat = jnp.concatenate([_col_blocks(wg[:, :d], MERGE_COL_TILE),
                                  _col_blocks(wg[:, d:], MERGE_COL_TILE)], axis=-1)
        h = _merge(h, row(mix_norm), z, b_gate, o, conv_w[l], wg_cat,
                   _bf16(w_branch_a[l]), _bf16(w_branch_b[l]), _bf16(w_out[l]), seq)

        h = _ffn(h, row(ffn2_norm), _bf16(ffn2_w1[l]), _bf16(ffn2_w3[l]), _bf16(ffn2_w2[l]))
        h = _ple(h, row(ple_norm), p[l].reshape(n, -1), _bf16(w_ple_gate[l]), _bf16(w_ple_proj[l]))
    return h.reshape(batch, seq, d)
```

```python
import functools
import math

import jax
import jax.numpy as jnp
from jax import lax
from jax.experimental import pallas as pl
from jax.experimental.pallas import tpu as pltpu

EPS = 1e-6
N_BUCKETS = 32
MAX_DISTANCE = 128
ATT_HEADS = 4
ATT_HEAD_DIM = 128
HEAD_W = 2 * ATT_HEAD_DIM
LANES = 128
NEG_INIT = -1e30
LOG2E = 1.0 / math.log(2.0)
Q_SCALE = LOG2E * ATT_HEAD_DIM ** -0.5

FIXED_SHIFT_MAX_SPAN = 100.0
BF16_ROUNDING_MARGIN = 1.01

VMEM_LIMIT_BYTES = 56 * 1024 * 1024

ROW_TILE = 512
PROJ_ROW_TILE = 1024
FFN_ROW_TILE = 1024
FF_TILE = 512
ATT_TILE = 512
ATT_GROUP = 16
ATT_FAR_KEYS = 2048
BIAS_TILE_REACH = 2
MERGE_COL_TILE = 512

_LOG_BUCKET_STARTS = (12, 16, 23, 32, 46, 64, 91)
FAR_DISTANCE = _LOG_BUCKET_STARTS[-1]


def _dot(a, b):
    return jnp.dot(a, b, preferred_element_type=jnp.float32)


def _dot_nt(a, b):
    return lax.dot_general(a, b, (((1,), (1,)), ((), ())),
                           preferred_element_type=jnp.float32)


def _rms(x, g):
    ms = jnp.mean(x * x, axis=-1, keepdims=True)
    return x * lax.rsqrt(ms + EPS) * g


def _params(*sem):
    return pltpu.CompilerParams(dimension_semantics=sem,
                                vmem_limit_bytes=VMEM_LIMIT_BYTES)


def _ffn_kernel(x_ref, g_ref, w1_ref, w3_ref, w2_ref, o_ref, xn_ref):
    j = pl.program_id(1)

    def half_swiglu(xn):
        gate = _dot(xn, w1_ref[...])
        up = _dot(xn, w3_ref[...])
        act = (0.5 * gate * jax.nn.sigmoid(gate) * up).astype(jnp.bfloat16)
        return _dot(act, w2_ref[...])

    @pl.when(j == 0)
    def _():
        x = x_ref[...]
        xn = _rms(x, g_ref[...]).astype(jnp.bfloat16)
        xn_ref[...] = xn
        o_ref[...] = x + half_swiglu(xn)

    @pl.when(j != 0)
    def _():
        o_ref[...] += half_swiglu(xn_ref[...])


def _ffn(x, g, w1, w3, w2):
    n, d = x.shape
    f = w1.shape[1]
    tm, tf = FFN_ROW_TILE, FF_TILE
    return pl.pallas_call(
        _ffn_kernel,
        grid=(n // tm, f // tf),
        in_specs=[
            pl.BlockSpec((tm, d), lambda i, j: (i, 0)),
            pl.BlockSpec((1, d), lambda i, j: (0, 0)),
            pl.BlockSpec((d, tf), lambda i, j: (0, j)),
            pl.BlockSpec((d, tf), lambda i, j: (0, j)),
            pl.BlockSpec((tf, d), lambda i, j: (j, 0)),
        ],
        out_specs=pl.BlockSpec((tm, d), lambda i, j: (i, 0)),
        out_shape=jax.ShapeDtypeStruct((n, d), jnp.float32),
        scratch_shapes=[pltpu.VMEM((tm, d), jnp.bfloat16)],
        compiler_params=_params("parallel", "arbitrary"),
        name="ffn",
    )(x, g, w1, w3, w2)


def _proj_kernel(h_ref, g_ref, wa_ref, wc_ref, wb_ref, wq_ref, wk_ref, wv_ref, qg_ref, kg_ref,
                 u_ref, z_ref, b_ref, q_ref, k_ref, v_ref):
    d = ATT_HEAD_DIM

    def project(u):
        z_ref[...] = _dot(u, wc_ref[...]) * _dot(u, wa_ref[...])
        b_ref[...] = _dot(u, wb_ref[...])
        yq = _dot(u, wq_ref[...])
        yk = _dot(u, wk_ref[...])
        for c in range(2):
            cols = slice(c * d, (c + 1) * d)
            q_ref[:, cols] = (_rms(yq[:, cols], qg_ref[...]) * Q_SCALE).astype(jnp.bfloat16)
            k_ref[:, cols] = _rms(yk[:, cols], kg_ref[...]).astype(jnp.bfloat16)
        v_ref[...] = _dot(u, wv_ref[...]).astype(jnp.bfloat16)

    @pl.when(pl.program_id(1) == 0)
    def _():
        u = _rms(h_ref[...], g_ref[...]).astype(jnp.bfloat16)
        u_ref[...] = u
        project(u)

    @pl.when(pl.program_id(1) != 0)
    def _():
        project(u_ref[...])


def _proj(h, g, w_in, q_gain, k_gain):
    n, d = h.shape
    tm, w = PROJ_ROW_TILE, HEAD_W
    nblk = w_in.shape[1] // (6 * w)
    col = pl.BlockSpec((tm, w), lambda i, j: (i, j))
    wspec = lambda p: pl.BlockSpec((d, w), lambda i, j: (0, p * nblk + j))
    gain = pl.BlockSpec((1, ATT_HEAD_DIM), lambda i, j: (0, 0))
    f32 = jax.ShapeDtypeStruct((n, nblk * w), jnp.float32)
    bf16 = jax.ShapeDtypeStruct((n, nblk * w), jnp.bfloat16)
    return pl.pallas_call(
        _proj_kernel,
        grid=(n // tm, nblk),
        in_specs=[
            pl.BlockSpec((tm, d), lambda i, j: (i, 0)),
            pl.BlockSpec((1, d), lambda i, j: (0, 0)),
            wspec(0), wspec(1), wspec(2), wspec(3), wspec(4), wspec(5),
            gain, gain,
        ],
        out_specs=[pl.BlockSpec((tm, d), lambda i, j: (i, 0)), col, col, col, col, col],
        out_shape=[jax.ShapeDtypeStruct((n, d), jnp.bfloat16), f32, f32, bf16, bf16, bf16],
        compiler_params=_params("parallel", "arbitrary"),
        name="proj",
    )(h, g, w_in, w_in, w_in, w_in, w_in, w_in, q_gain, k_gain)


def _bias_tile_kernel(tab_ref, o_ref):
    h = pl.program_id(0)
    t = o_ref.shape[-1]
    offset = (pl.program_id(1) - BIAS_TILE_REACH) * t
    rel = (lax.broadcasted_iota(jnp.int32, (t, t), 1)
           - lax.broadcasted_iota(jnp.int32, (t, t), 0) + offset)
    n = jnp.abs(rel)
    nb = N_BUCKETS // 2
    max_exact = nb // 2
    large = jnp.full((t, t), max_exact, jnp.int32)
    for start in _LOG_BUCKET_STARTS:
        large = large + (n >= start).astype(jnp.int32)
    bucket = jnp.where(rel > 0, nb, 0) + jnp.where(n < max_exact, n, large)
    acc = jnp.zeros((t, t), jnp.float32)
    for bkt in range(N_BUCKETS):
        acc = jnp.where(bucket == bkt, tab_ref[bkt * ATT_HEADS + h], acc)
    o_ref[0, 0] = acc


def _bias_tiles(table_flat, t):
    n_off = 2 * BIAS_TILE_REACH + 1
    return pl.pallas_call(
        _bias_tile_kernel,
        grid=(ATT_HEADS, n_off),
        in_specs=[pl.BlockSpec(memory_space=pltpu.SMEM)],
        out_specs=pl.BlockSpec((1, 1, t, t), lambda h, o: (h, o, 0, 0)),
        out_shape=jax.ShapeDtypeStruct((ATT_HEADS, n_off, t, t), jnp.float32),
        compiler_params=_params("parallel", "parallel"),
        name="bias_tiles",
    )(table_flat)


def _attn_kernel(tab_ref, fixed_ref, lam_ref, q_ref, k_ref, v_ref, bias_ref, sg_ref, o_ref,
                 m_ref, l_ref, lp_ref, acc_ref, *, lam_init):
    h = pl.program_id(1)
    qi = pl.program_id(2)
    t = q_ref.shape[0]
    d = ATT_HEAD_DIM

    nb = N_BUCKETS // 2
    bias_left = tab_ref[(nb - 1) * ATT_HEADS + h]
    bias_right = tab_ref[(N_BUCKETS - 1) * ATT_HEADS + h]

    def logits(start, size, c):
        rows = pl.ds(pl.multiple_of(start, t), size)
        return _dot_nt(q_ref[:, c * d:(c + 1) * d], k_ref[rows, c * d:(c + 1) * d])

    def values(start, size):
        return v_ref[pl.ds(pl.multiple_of(start, t), size), :]

    def fixed_step(start, size, bias):
        v = values(start, size)
        for c in range(2):
            p = jnp.exp2(logits(start, size, c) + bias)
            part = p[:, 0:LANES]
            for s in range(1, size // LANES):
                part = part + p[:, s * LANES:(s + 1) * LANES]
            lp_ref[c] += part
            acc_ref[c] += _dot(p.astype(jnp.bfloat16), v)

    def online_step(start, size, bias):
        v = values(start, size)
        for c in range(2):
            x = logits(start, size, c) + bias
            m_old = m_ref[c]
            m_new = jnp.maximum(m_old, jnp.max(x, axis=-1, keepdims=True))
            alpha = jnp.exp2(m_old - m_new)
            p = jnp.exp2(x - m_new)
            l_ref[c] = alpha * l_ref[c] + jnp.sum(p, axis=-1, keepdims=True)
            acc_ref[c] = alpha * acc_ref[c] + _dot(p.astype(jnp.bfloat16), v)
            m_ref[c] = m_new

    def sweep(step, group, far_size):
        def group_body(g, carry):
            first = g * group
            near = jnp.logical_and(qi >= first - 1, qi <= first + group)

            @pl.when(near)
            def _():
                for s in range(group):
                    off = jnp.clip(first + s - qi, -BIAS_TILE_REACH, BIAS_TILE_REACH)
                    step((first + s) * t, t, bias_ref[0, off + BIAS_TILE_REACH])

            @pl.when(jnp.logical_not(near))
            def _():
                bias = jnp.where(first < qi, bias_left, bias_right)
                for s in range(group * t // far_size):
                    step(first * t + s * far_size, far_size, bias)

            return carry

        lax.fori_loop(0, k_ref.shape[0] // (group * t), group_body, 0)

    acc_ref[...] = jnp.zeros_like(acc_ref)

    @pl.when(fixed_ref[0] == 1)
    def _():
        lp_ref[...] = jnp.zeros_like(lp_ref)
        sweep(fixed_step, ATT_GROUP, ATT_FAR_KEYS)
        for c in range(2):
            l_ref[c] = jnp.sum(lp_ref[c], axis=-1, keepdims=True)

    @pl.when(fixed_ref[0] != 1)
    def _():
        m_ref[...] = jnp.full_like(m_ref, NEG_INIT)
        l_ref[...] = jnp.zeros_like(l_ref)
        sweep(online_step, 1, t)

    lam_q1, lam_k1, lam_q2, lam_k2 = (lam_ref[r:r + 1, :] for r in range(4))
    lam = (jnp.exp(jnp.sum(lam_q1 * lam_k1, axis=-1, keepdims=True))
           - jnp.exp(jnp.sum(lam_q2 * lam_k2, axis=-1, keepdims=True)) + lam_init)
    o = acc_ref[0] / l_ref[0] - lam * (acc_ref[1] / l_ref[1])
    o_ref[...] = (_rms(o, sg_ref[...]) * (1.0 - lam_init)).astype(jnp.bfloat16)


def _attn(table_flat, fixed_flag, lam_rows, q, k, v, bias_tiles, sub_gain, batch, seq, lam_init):
    n, width = q.shape
    t = ATT_TILE
    assert FAR_DISTANCE <= (BIAS_TILE_REACH - 1) * t + 1
    assert seq % (ATT_GROUP * t) == 0 and (ATT_GROUP * t) % ATT_FAR_KEYS == 0
    nq = seq // t
    w = HEAD_W
    return pl.pallas_call(
        functools.partial(_attn_kernel, lam_init=lam_init),
        grid=(batch, ATT_HEADS, nq),
        in_specs=[
            pl.BlockSpec(memory_space=pltpu.SMEM),
            pl.BlockSpec(memory_space=pltpu.SMEM),
            pl.BlockSpec((4, ATT_HEAD_DIM), lambda b, h, i: (0, 0)),
            pl.BlockSpec((t, w), lambda b, h, i: (b * nq + i, h)),
            pl.BlockSpec((seq, w), lambda b, h, i: (b, h)),
            pl.BlockSpec((seq, w), lambda b, h, i: (b, h)),
            pl.BlockSpec((1, 2 * BIAS_TILE_REACH + 1, t, t), lambda b, h, i: (h, 0, 0, 0)),
            pl.BlockSpec((1, w), lambda b, h, i: (0, 0)),
        ],
        out_specs=pl.BlockSpec((t, w), lambda b, h, i: (b * nq + i, h)),
        out_shape=jax.ShapeDtypeStruct((n, width), jnp.bfloat16),
        scratch_shapes=[
            pltpu.VMEM((2, t, 1), jnp.float32),
            pltpu.VMEM((2, t, 1), jnp.float32),
            pltpu.VMEM((2, t, LANES), jnp.float32),
            pltpu.VMEM((2, t, w), jnp.float32),
        ],
        compiler_params=_params("parallel", "parallel", "arbitrary"),
        name="attn",
    )(table_flat, fixed_flag, lam_rows, q, k, v, bias_tiles, sub_gain)


def _shifted_bias_table(rel_bias, q_gain, k_gain):
    table = rel_bias.astype(jnp.float32) * LOG2E
    dot_bound = (Q_SCALE * ATT_HEAD_DIM * BF16_ROUNDING_MARGIN
                 * jnp.max(jnp.abs(q_gain)) * jnp.max(jnp.abs(k_gain)))
    span = 2.0 * dot_bound + (jnp.max(table) - jnp.min(table))
    fixed = (span <= FIXED_SHIFT_MAX_SPAN).astype(jnp.int32).reshape(1)
    shift = jnp.where(fixed[0] == 1, dot_bound + jnp.max(table), 0.0)
    return (table - shift).reshape(-1), fixed


def _merge_kernel(h_ref, u_ref, z_ref, zp_ref, zn_ref, b_ref, o_ref, cw_ref,
                  wga_ref, wgb_ref, wa_ref, wb_ref, wo_ref, out_ref, a_ref, *, tiles_per_seq):
    i = pl.program_id(0)
    j = pl.program_id(1)
    tm = h_ref.shape[0]
    halo = zp_ref.shape[0]

    def mix(a):
        u = u_ref[...]
        g_a = jax.nn.sigmoid(_dot(u, wga_ref[...]))
        g_b = jax.nn.sigmoid(_dot(u, wgb_ref[...]))
        y_a = _dot(a, wa_ref[...])
        y_b = _dot(o_ref[...], wb_ref[...])
        return _dot((g_a * y_a + g_b * y_b).astype(jnp.bfloat16), wo_ref[...])

    @pl.when(j == 0)
    def _():
        z = z_ref[...]
        pos = i % tiles_per_seq
        before = jnp.where(pos == 0, 0.0, zp_ref[halo - 1:halo, :])
        after = jnp.where(pos == tiles_per_seq - 1, 0.0, zn_ref[0:1, :])
        row = lax.broadcasted_iota(jnp.int32, z.shape, 0)
        z_prev = jnp.where(row == 0, before, pltpu.roll(z, 1, 0))
        z_next = jnp.where(row == tm - 1, after, pltpu.roll(z, tm - 1, 0))
        conv = z_prev * cw_ref[0:1, :] + z * cw_ref[1:2, :] + z_next * cw_ref[2:3, :]
        a = (b_ref[...] * conv).astype(jnp.bfloat16)
        a_ref[...] = a
        out_ref[...] = h_ref[...] + mix(a)

    @pl.when(j != 0)
    def _():
        out_ref[...] += mix(a_ref[...])


def _merge(h, u, z, b, o, conv_w, wg, wa, wb, wo, seq):
    n, d = h.shape
    cw = z.shape[1]
    tm, tn = ROW_TILE, MERGE_COL_TILE
    halo = 8
    nj = d // tn
    hb = tm // halo
    last_hb = n // halo - 1
    rows = lambda width: pl.BlockSpec((tm, width), lambda i, j: (i, 0))
    return pl.pallas_call(
        functools.partial(_merge_kernel, tiles_per_seq=seq // tm),
        grid=(n // tm, nj),
        in_specs=[
            rows(d),
            rows(d),
            rows(cw),
            pl.BlockSpec((halo, cw), lambda i, j: (jnp.maximum(i * hb - 1, 0), 0)),
            pl.BlockSpec((halo, cw), lambda i, j: (jnp.minimum((i + 1) * hb, last_hb), 0)),
            rows(cw),
            rows(o.shape[1]),
            pl.BlockSpec((3, cw), lambda i, j: (0, 0)),
            pl.BlockSpec((d, tn), lambda i, j: (0, j)),
            pl.BlockSpec((d, tn), lambda i, j: (0, nj + j)),
            pl.BlockSpec((cw, tn), lambda i, j: (0, j)),
            pl.BlockSpec((o.shape[1], tn), lambda i, j: (0, j)),
            pl.BlockSpec((tn, d), lambda i, j: (j, 0)),
        ],
        out_specs=rows(d),
        out_shape=jax.ShapeDtypeStruct((n, d), jnp.float32),
        scratch_shapes=[pltpu.VMEM((tm, cw), jnp.bfloat16)],
        compiler_params=_params("parallel", "arbitrary"),
        name="merge",
    )(h, u, z, z, z, b, o, conv_w, wg, wg, wa, wb, wo)


def _ple_kernel(h_ref, g_ref, p_ref, wg_ref, wp_ref, o_ref):
    h = h_ref[...]
    gate = jax.nn.sigmoid(_dot(_rms(h, g_ref[...]).astype(jnp.bfloat16), wg_ref[...]))
    o_ref[...] = h + gate * _dot(p_ref[...].astype(jnp.bfloat16), wp_ref[...])


def _ple(h, g, p, wg, wp):
    n, d = h.shape
    e = p.shape[1]
    tm = ROW_TILE
    return pl.pallas_call(
        _ple_kernel,
        grid=(n // tm,),
        in_specs=[
            pl.BlockSpec((tm, d), lambda i: (i, 0)),
            pl.BlockSpec((1, d), lambda i: (0, 0)),
            pl.BlockSpec((tm, e), lambda i: (i, 0)),
            pl.BlockSpec((d, d), lambda i: (0, 0)),
            pl.BlockSpec((e, d), lambda i: (0, 0)),
        ],
        out_specs=pl.BlockSpec((tm, d), lambda i: (i, 0)),
        out_shape=jax.ShapeDtypeStruct((n, d), jnp.float32),
        compiler_params=_params("parallel"),
        name="ple",
    )(h, g, p, wg, wp)


def _bf16(w):
    return w.astype(jnp.bfloat16)


def kernel(x, p, ffn1_norm, ffn1_w1, ffn1_w3, ffn1_w2, mix_norm, w_in, conv_w, q_norm, k_norm,
           lam_q1, lam_k1, lam_q2, lam_k2, sub_norm, rel_bias, w_branch_a, w_branch_b, w_gate,
           w_out, ffn2_norm, ffn2_w1, ffn2_w3, ffn2_w2, ple_norm, w_ple_gate, w_ple_proj):
    batch, seq, d = x.shape
    n = batch * seq
    depth = ffn1_w1.shape[0]
    h = x.reshape(n, d)

    for l in range(depth):
        row = lambda a: a[l].reshape(1, -1)
        h = _ffn(h, row(ffn1_norm), _bf16(ffn1_w1[l]), _bf16(ffn1_w3[l]), _bf16(ffn1_w2[l]))

        u, z, b_gate, q, k, v = _proj(h, row(mix_norm), _bf16(w_in[l]), row(q_norm), row(k_norm))

        lam_init = 0.8 - 0.6 * math.exp(-0.3 * l)
        lam_rows = jnp.stack([lam_q1[l], lam_k1[l], lam_q2[l], lam_k2[l]])
        table_flat, fixed_flag = _shifted_bias_table(rel_bias, q_norm[l], k_norm[l])
        bias_tiles = _bias_tiles(table_flat, ATT_TILE)
        o = _attn(table_flat, fixed_flag, lam_rows, q, k, v, bias_tiles, row(sub_norm),
                  batch, seq, lam_init)

        h = _merge(h, u, z, b_gate, o, conv_w[l], _bf16(w_gate[l]),
                   _bf16(w_branch_a[l]), _bf16(w_branch_b[l]), _bf16(w_out[l]), seq)

        h = _ffn(h, row(ffn2_norm), _bf16(ffn2_w1[l]), _bf16(ffn2_w3[l]), _bf16(ffn2_w2[l]))
        h = _ple(h, row(ple_norm), p[l].reshape(n, -1), _bf16(w_ple_gate[l]), _bf16(w_ple_proj[l]))
    return h.reshape(batch, seq, d)
```

```python
import functools
import math

import jax
import jax.numpy as jnp
from jax import lax
from jax.experimental import pallas as pl
from jax.experimental.pallas import tpu as pltpu

EPS = 1e-6
N_BUCKETS = 32
MAX_DISTANCE = 128
ATT_HEADS = 4
ATT_HEAD_DIM = 128
HEAD_W = 2 * ATT_HEAD_DIM
LANES = 128
NEG_INIT = -1e30
LOG2E = 1.0 / math.log(2.0)
Q_SCALE = LOG2E * ATT_HEAD_DIM ** -0.5

FIXED_SHIFT_MAX_SPAN = 100.0
BF16_ROUNDING_MARGIN = 1.01

VMEM_LIMIT_BYTES = 56 * 1024 * 1024

ROW_TILE = 512
PROJ_ROW_TILE = 1024
FFN_ROW_TILE = 1024
FF_TILE = 512
ATT_TILE = 512
BIAS_TILE_REACH = 2
MERGE_COL_TILE = 512

_LOG_BUCKET_STARTS = (12, 16, 23, 32, 46, 64, 91)
FAR_DISTANCE = _LOG_BUCKET_STARTS[-1]


def _dot(a, b):
    return jnp.dot(a, b, preferred_element_type=jnp.float32)


def _dot_nt(a, b):
    return lax.dot_general(a, b, (((1,), (1,)), ((), ())),
                           preferred_element_type=jnp.float32)


def _rms(x, g):
    ms = jnp.mean(x * x, axis=-1, keepdims=True)
    return x * lax.rsqrt(ms + EPS) * g


def _early_rows(tile, width, n_tiles, switch_step):
    def index(i, j):
        return (jnp.where(j < switch_step, i, jnp.minimum(i + 1, n_tiles - 1)), 0)
    return pl.BlockSpec((tile, width), index)


def _params(*sem):
    return pltpu.CompilerParams(dimension_semantics=sem,
                                vmem_limit_bytes=VMEM_LIMIT_BYTES)


def _ffn_kernel(x_ref, g_ref, w1_ref, w3_ref, w2_ref, o_ref, xn_ref):
    j = pl.program_id(1)

    def half_swiglu(xn):
        gate = _dot(xn, w1_ref[...])
        up = _dot(xn, w3_ref[...])
        act = (0.5 * gate * jax.nn.sigmoid(gate) * up).astype(jnp.bfloat16)
        return _dot(act, w2_ref[...])

    @pl.when(j == 0)
    def _():
        x = x_ref[...]
        xn = _rms(x, g_ref[...]).astype(jnp.bfloat16)
        xn_ref[...] = xn
        o_ref[...] = x + half_swiglu(xn)

    @pl.when(j != 0)
    def _():
        o_ref[...] += half_swiglu(xn_ref[...])


def _ffn(x, g, w1, w3, w2):
    n, d = x.shape
    f = w1.shape[1]
    tm, tf = FFN_ROW_TILE, FF_TILE
    return pl.pallas_call(
        _ffn_kernel,
        grid=(n // tm, f // tf),
        in_specs=[
            _early_rows(tm, d, n // tm, f // tf // 2),
            pl.BlockSpec((1, d), lambda i, j: (0, 0)),
            pl.BlockSpec((d, tf), lambda i, j: (0, j)),
            pl.BlockSpec((d, tf), lambda i, j: (0, j)),
            pl.BlockSpec((tf, d), lambda i, j: (j, 0)),
        ],
        out_specs=pl.BlockSpec((tm, d), lambda i, j: (i, 0)),
        out_shape=jax.ShapeDtypeStruct((n, d), jnp.float32),
        scratch_shapes=[pltpu.VMEM((tm, d), jnp.bfloat16)],
        compiler_params=_params("parallel", "arbitrary"),
        name="ffn",
    )(x, g, w1, w3, w2)


def _proj_kernel(h_ref, g_ref, wa_ref, wc_ref, wb_ref, wq_ref, wk_ref, wv_ref, qg_ref, kg_ref,
                 u_ref, z_ref, b_ref, q_ref, k_ref, v_ref):
    d = ATT_HEAD_DIM

    def project(u):
        z_ref[...] = _dot(u, wc_ref[...]) * _dot(u, wa_ref[...])
        b_ref[...] = _dot(u, wb_ref[...])
        yq = _dot(u, wq_ref[...])
        yk = _dot(u, wk_ref[...])
        for c in range(2):
            cols = slice(c * d, (c + 1) * d)
            q_ref[:, cols] = (_rms(yq[:, cols], qg_ref[...]) * Q_SCALE).astype(jnp.bfloat16)
            k_ref[:, cols] = _rms(yk[:, cols], kg_ref[...]).astype(jnp.bfloat16)
        v_ref[...] = _dot(u, wv_ref[...]).astype(jnp.bfloat16)

    @pl.when(pl.program_id(1) == 0)
    def _():
        u = _rms(h_ref[...], g_ref[...]).astype(jnp.bfloat16)
        u_ref[...] = u
        project(u)

    @pl.when(pl.program_id(1) != 0)
    def _():
        project(u_ref[...])


def _proj(h, g, w_in, q_gain, k_gain):
    n, d = h.shape
    tm, w = PROJ_ROW_TILE, HEAD_W
    nblk = w_in.shape[1] // (6 * w)
    col = pl.BlockSpec((tm, w), lambda i, j: (i, j))
    wspec = lambda p: pl.BlockSpec((d, w), lambda i, j: (0, p * nblk + j))
    gain = pl.BlockSpec((1, ATT_HEAD_DIM), lambda i, j: (0, 0))
    f32 = jax.ShapeDtypeStruct((n, nblk * w), jnp.float32)
    bf16 = jax.ShapeDtypeStruct((n, nblk * w), jnp.bfloat16)
    return pl.pallas_call(
        _proj_kernel,
        grid=(n // tm, nblk),
        in_specs=[
            _early_rows(tm, d, n // tm, nblk // 2),
            pl.BlockSpec((1, d), lambda i, j: (0, 0)),
            wspec(0), wspec(1), wspec(2), wspec(3), wspec(4), wspec(5),
            gain, gain,
        ],
        out_specs=[pl.BlockSpec((tm, d), lambda i, j: (i, 0)), col, col, col, col, col],
        out_shape=[jax.ShapeDtypeStruct((n, d), jnp.bfloat16), f32, f32, bf16, bf16, bf16],
        compiler_params=_params("parallel", "arbitrary"),
        name="proj",
    )(h, g, w_in, w_in, w_in, w_in, w_in, w_in, q_gain, k_gain)


def _bias_tile_kernel(tab_ref, o_ref):
    h = pl.program_id(0)
    t = o_ref.shape[-1]
    offset = (pl.program_id(1) - BIAS_TILE_REACH) * t
    rel = (lax.broadcasted_iota(jnp.int32, (t, t), 1)
           - lax.broadcasted_iota(jnp.int32, (t, t), 0) + offset)
    n = jnp.abs(rel)
    nb = N_BUCKETS // 2
    max_exact = nb // 2
    large = jnp.full((t, t), max_exact, jnp.int32)
    for start in _LOG_BUCKET_STARTS:
        large = large + (n >= start).astype(jnp.int32)
    bucket = jnp.where(rel > 0, nb, 0) + jnp.where(n < max_exact, n, large)
    acc = jnp.zeros((t, t), jnp.float32)
    for bkt in range(N_BUCKETS):
        acc = jnp.where(bucket == bkt, tab_ref[bkt * ATT_HEADS + h], acc)
    o_ref[0, 0] = acc


def _bias_tiles(table_flat, t):
    n_off = 2 * BIAS_TILE_REACH + 1
    return pl.pallas_call(
        _bias_tile_kernel,
        grid=(ATT_HEADS, n_off),
        in_specs=[pl.BlockSpec(memory_space=pltpu.SMEM)],
        out_specs=pl.BlockSpec((1, 1, t, t), lambda h, o: (h, o, 0, 0)),
        out_shape=jax.ShapeDtypeStruct((ATT_HEADS, n_off, t, t), jnp.float32),
        compiler_params=_params("parallel", "parallel"),
        name="bias_tiles",
    )(table_flat)


def _attn_kernel(fixed_ref, lam_ref, q_ref, k_ref, v_ref, bias_ref, sg_ref, o_ref,
                 m_ref, l_ref, acc_ref, *, lam_init):
    qi = pl.program_id(2)
    t = q_ref.shape[0]
    d = ATT_HEAD_DIM

    def bias_tile(kc):
        off = jnp.clip(kc - qi, -BIAS_TILE_REACH, BIAS_TILE_REACH)
        return bias_ref[0, off + BIAS_TILE_REACH]

    def finish(acc, l):
        lam_q1, lam_k1, lam_q2, lam_k2 = (lam_ref[r:r + 1, :] for r in range(4))
        lam = (jnp.exp(jnp.sum(lam_q1 * lam_k1, axis=-1, keepdims=True))
               - jnp.exp(jnp.sum(lam_q2 * lam_k2, axis=-1, keepdims=True)) + lam_init)
        o = acc[0] / l[0] - lam * (acc[1] / l[1])
        o_ref[...] = (_rms(o, sg_ref[...]) * (1.0 - lam_init)).astype(jnp.bfloat16)

    def fixed_shift_tile():
        acc = [jnp.zeros((t, HEAD_W), jnp.float32) for _ in range(2)]
        lp = [jnp.zeros((t, LANES), jnp.float32) for _ in range(2)]
        for kc in range(k_ref.shape[0] // t):
            bias = bias_tile(kc)
            v = v_ref[kc * t:(kc + 1) * t, :]
            for c in range(2):
                cols = slice(c * d, (c + 1) * d)
                p = jnp.exp2(_dot_nt(q_ref[:, cols], k_ref[kc * t:(kc + 1) * t, cols]) + bias)
                for s in range(t // LANES):
                    lp[c] = lp[c] + p[:, s * LANES:(s + 1) * LANES]
                acc[c] = acc[c] + _dot(p.astype(jnp.bfloat16), v)
        finish(acc, [jnp.sum(x, axis=-1, keepdims=True) for x in lp])

    def online_step(kc):
        rows = pl.ds(pl.multiple_of(kc * t, t), t)
        bias = bias_tile(kc)
        v = v_ref[rows, :]
        for c in range(2):
            cols = slice(c * d, (c + 1) * d)
            x = _dot_nt(q_ref[:, cols], k_ref[rows, cols]) + bias
            m_old = m_ref[c]
            m_new = jnp.maximum(m_old, jnp.max(x, axis=-1, keepdims=True))
            alpha = jnp.exp2(m_old - m_new)
            p = jnp.exp2(x - m_new)
            l_ref[c] = alpha * l_ref[c] + jnp.sum(p, axis=-1, keepdims=True)
            acc_ref[c] = alpha * acc_ref[c] + _dot(p.astype(jnp.bfloat16), v)
            m_ref[c] = m_new

    def online_tile():
        m_ref[...] = jnp.full_like(m_ref, NEG_INIT)
        l_ref[...] = jnp.zeros_like(l_ref)
        acc_ref[...] = jnp.zeros_like(acc_ref)

        def body(kc, carry):
            online_step(kc)
            return carry

        lax.fori_loop(0, k_ref.shape[0] // t, body, 0)
        finish([acc_ref[0], acc_ref[1]], [l_ref[0], l_ref[1]])

    pl.when(fixed_ref[0] == 1)(fixed_shift_tile)
    pl.when(fixed_ref[0] != 1)(online_tile)


def _attn(fixed_flag, lam_rows, q, k, v, bias_tiles, sub_gain, batch, seq, lam_init):
    n, width = q.shape
    t = ATT_TILE
    assert FAR_DISTANCE <= (BIAS_TILE_REACH - 1) * t + 1 and seq % t == 0
    nq = seq // t
    w = HEAD_W
    return pl.pallas_call(
        functools.partial(_attn_kernel, lam_init=lam_init),
        grid=(batch, ATT_HEADS, nq),
        in_specs=[
            pl.BlockSpec(memory_space=pltpu.SMEM),
            pl.BlockSpec((4, ATT_HEAD_DIM), lambda b, h, i: (0, 0)),
            pl.BlockSpec((t, w), lambda b, h, i: (b * nq + i, h)),
            pl.BlockSpec((seq, w), lambda b, h, i: (b, h)),
            pl.BlockSpec((seq, w), lambda b, h, i: (b, h)),
            pl.BlockSpec((1, 2 * BIAS_TILE_REACH + 1, t, t), lambda b, h, i: (h, 0, 0, 0)),
            pl.BlockSpec((1, w), lambda b, h, i: (0, 0)),
        ],
        out_specs=pl.BlockSpec((t, w), lambda b, h, i: (b * nq + i, h)),
        out_shape=jax.ShapeDtypeStruct((n, width), jnp.bfloat16),
        scratch_shapes=[
            pltpu.VMEM((2, t, 1), jnp.float32),
            pltpu.VMEM((2, t, 1), jnp.float32),
            pltpu.VMEM((2, t, w), jnp.float32),
        ],
        compiler_params=_params("parallel", "parallel", "arbitrary"),
        name="attn",
    )(fixed_flag, lam_rows, q, k, v, bias_tiles, sub_gain)


def _shifted_bias_table(rel_bias, q_gain, k_gain):
    table = rel_bias.astype(jnp.float32) * LOG2E
    dot_bound = (Q_SCALE * ATT_HEAD_DIM * BF16_ROUNDING_MARGIN
                 * jnp.max(jnp.abs(q_gain)) * jnp.max(jnp.abs(k_gain)))
    span = 2.0 * dot_bound + (jnp.max(table) - jnp.min(table))
    fixed = (span <= FIXED_SHIFT_MAX_SPAN).astype(jnp.int32).reshape(1)
    shift = jnp.where(fixed[0] == 1, dot_bound + jnp.max(table), 0.0)
    return (table - shift).reshape(-1), fixed


def _merge_kernel(h_ref, u_ref, z_ref, zp_ref, zn_ref, b_ref, o_ref, cw_ref,
                  wga_ref, wgb_ref, wa_ref, wb_ref, wo_ref, out_ref, a_ref, *, tiles_per_seq):
    i = pl.program_id(0)
    j = pl.program_id(1)
    tm = h_ref.shape[0]
    halo = zp_ref.shape[0]

    def mix(a):
        u = u_ref[...]
        g_a = jax.nn.sigmoid(_dot(u, wga_ref[...]))
        g_b = jax.nn.sigmoid(_dot(u, wgb_ref[...]))
        y_a = _dot(a, wa_ref[...])
        y_b = _dot(o_ref[...], wb_ref[...])
        return _dot((g_a * y_a + g_b * y_b).astype(jnp.bfloat16), wo_ref[...])

    @pl.when(j == 0)
    def _():
        z = z_ref[...]
        pos = i % tiles_per_seq
        before = jnp.where(pos == 0, 0.0, zp_ref[halo - 1:halo, :])
        after = jnp.where(pos == tiles_per_seq - 1, 0.0, zn_ref[0:1, :])
        row = lax.broadcasted_iota(jnp.int32, z.shape, 0)
        z_prev = jnp.where(row == 0, before, pltpu.roll(z, 1, 0))
        z_next = jnp.where(row == tm - 1, after, pltpu.roll(z, tm - 1, 0))
        conv = z_prev * cw_ref[0:1, :] + z * cw_ref[1:2, :] + z_next * cw_ref[2:3, :]
        a = (b_ref[...] * conv).astype(jnp.bfloat16)
        a_ref[...] = a
        out_ref[...] = h_ref[...] + mix(a)

    @pl.when(j != 0)
    def _():
        out_ref[...] += mix(a_ref[...])


def _merge(h, u, z, b, o, conv_w, wg, wa, wb, wo, seq):
    n, d = h.shape
    cw = z.shape[1]
    tm, tn = ROW_TILE, MERGE_COL_TILE
    halo = 8
    nj = d // tn
    hb = tm // halo
    last_hb = n // halo - 1
    n_tiles = n // tm
    rows = lambda width: pl.BlockSpec((tm, width), lambda i, j: (i, 0))
    z_tile = lambda i, j: jnp.where(j < 1, i, jnp.minimum(i + 1, n_tiles - 1))
    return pl.pallas_call(
        functools.partial(_merge_kernel, tiles_per_seq=seq // tm),
        grid=(n // tm, nj),
        in_specs=[
            _early_rows(tm, d, n_tiles, 3),
            rows(d),
            _early_rows(tm, cw, n_tiles, 1),
            pl.BlockSpec((halo, cw), lambda i, j: (jnp.maximum(z_tile(i, j) * hb - 1, 0), 0)),
            pl.BlockSpec((halo, cw), lambda i, j: (jnp.minimum((z_tile(i, j) + 1) * hb, last_hb), 0)),
            _early_rows(tm, cw, n_tiles, 2),
            rows(o.shape[1]),
            pl.BlockSpec((3, cw), lambda i, j: (0, 0)),
            pl.BlockSpec((d, tn), lambda i, j: (0, j)),
            pl.BlockSpec((d, tn), lambda i, j: (0, nj + j)),
            pl.BlockSpec((cw, tn), lambda i, j: (0, j)),
            pl.BlockSpec((o.shape[1], tn), lambda i, j: (0, j)),
            pl.BlockSpec((tn, d), lambda i, j: (j, 0)),
        ],
        out_specs=rows(d),
        out_shape=jax.ShapeDtypeStruct((n, d), jnp.float32),
        scratch_shapes=[pltpu.VMEM((tm, cw), jnp.bfloat16)],
        compiler_params=_params("parallel", "arbitrary"),
        name="merge",
    )(h, u, z, z, z, b, o, conv_w, wg, wg, wa, wb, wo)


def _ple_kernel(h_ref, g_ref, p_ref, wg_ref, wp_ref, o_ref):
    h = h_ref[...]
    gate = jax.nn.sigmoid(_dot(_rms(h, g_ref[...]).astype(jnp.bfloat16), wg_ref[...]))
    o_ref[...] = h + gate * _dot(p_ref[...].astype(jnp.bfloat16), wp_ref[...])


def _ple(h, g, p, wg, wp):
    n, d = h.shape
    e = p.shape[1]
    tm = ROW_TILE
    return pl.pallas_call(
        _ple_kernel,
        grid=(n // tm,),
        in_specs=[
            pl.BlockSpec((tm, d), lambda i: (i, 0)),
            pl.BlockSpec((1, d), lambda i: (0, 0)),
            pl.BlockSpec((tm, e), lambda i: (i, 0)),
            pl.BlockSpec((d, d), lambda i: (0, 0)),
            pl.BlockSpec((e, d), lambda i: (0, 0)),
        ],
        out_specs=pl.BlockSpec((tm, d), lambda i: (i, 0)),
        out_shape=jax.ShapeDtypeStruct((n, d), jnp.float32),
        compiler_params=_params("parallel"),
        name="ple",
    )(h, g, p, wg, wp)


def _bf16(w):
    return w.astype(jnp.bfloat16)


def kernel(x, p, ffn1_norm, ffn1_w1, ffn1_w3, ffn1_w2, mix_norm, w_in, conv_w, q_norm, k_norm,
           lam_q1, lam_k1, lam_q2, lam_k2, sub_norm, rel_bias, w_branch_a, w_branch_b, w_gate,
           w_out, ffn2_norm, ffn2_w1, ffn2_w3, ffn2_w2, ple_norm, w_ple_gate, w_ple_proj):
    batch, seq, d = x.shape
    n = batch * seq
    depth = ffn1_w1.shape[0]
    h = x.reshape(n, d)

    for l in range(depth):
        row = lambda a: a[l].reshape(1, -1)
        h = _ffn(h, row(ffn1_norm), _bf16(ffn1_w1[l]), _bf16(ffn1_w3[l]), _bf16(ffn1_w2[l]))

        u, z, b_gate, q, k, v = _proj(h, row(mix_norm), _bf16(w_in[l]), row(q_norm), row(k_norm))

        lam_init = 0.8 - 0.6 * math.exp(-0.3 * l)
        lam_rows = jnp.stack([lam_q1[l], lam_k1[l], lam_q2[l], lam_k2[l]])
        table_flat, fixed_flag = _shifted_bias_table(rel_bias, q_norm[l], k_norm[l])
        bias_tiles = _bias_tiles(table_flat, ATT_TILE)
        o = _attn(fixed_flag, lam_rows, q, k, v, bias_tiles, row(sub_norm), batch, seq, lam_init)

        h = _merge(h, u, z, b_gate, o, conv_w[l], _bf16(w_gate[l]),
                   _bf16(w_branch_a[l]), _bf16(w_branch_b[l]), _bf16(w_out[l]), seq)

        h = _ffn(h, row(ffn2_norm), _bf16(ffn2_w1[l]), _bf16(ffn2_w3[l]), _bf16(ffn2_w2[l]))
        h = _ple(h, row(ple_norm), p[l].reshape(n, -1), _bf16(w_ple_gate[l]), _bf16(w_ple_proj[l]))
    return h.reshape(batch, seq, d)
```

```python
import functools
import math

import jax
import jax.numpy as jnp
from jax import lax
from jax.experimental import pallas as pl
from jax.experimental.pallas import tpu as pltpu

EPS = 1e-6
N_BUCKETS = 32
ATT_HEADS = 4
ATT_HEAD_DIM = 128
HEAD_W = 2 * ATT_HEAD_DIM
LANES = 128
NEG_INIT = -1e30
LOG2E = 1.0 / math.log(2.0)
Q_SCALE = LOG2E * ATT_HEAD_DIM ** -0.5

FIXED_SHIFT_MAX_SPAN = 100.0
BF16_ROUNDING_MARGIN = 1.01

VMEM_LIMIT_BYTES = 56 * 1024 * 1024

ROW_TILE = 512
PROJ_ROW_TILE = 1024
FFN_ROW_TILE = 1024
FF_TILE = 512
ATT_TILE = 512
BIAS_TILE_REACH = 2
MERGE_COL_TILE = 512

_LOG_BUCKET_STARTS = (12, 16, 23, 32, 46, 64, 91)
FAR_DISTANCE = _LOG_BUCKET_STARTS[-1]


def _dot(a, b):
    return jnp.dot(a, b, preferred_element_type=jnp.float32)


def _dot_nt(a, b):
    return lax.dot_general(a, b, (((1,), (1,)), ((), ())),
                           preferred_element_type=jnp.float32)


def _rms(x, g):
    ms = jnp.mean(x * x, axis=-1, keepdims=True)
    return x * lax.rsqrt(ms + EPS) * g


def _early_rows(tile, width, n_tiles, switch_step):
    def index(i, j):
        return (jnp.where(j < switch_step, i, jnp.minimum(i + 1, n_tiles - 1)), 0)
    return pl.BlockSpec((tile, width), index)


def _params(*sem):
    return pltpu.CompilerParams(dimension_semantics=sem,
                                vmem_limit_bytes=VMEM_LIMIT_BYTES)


def _ffn_kernel(x_ref, g_ref, w1_ref, w3_ref, w2_ref, o_ref, xn_ref):
    j = pl.program_id(1)

    def half_swiglu(xn):
        gate = _dot(xn, w1_ref[...])
        up = _dot(xn, w3_ref[...])
        act = (0.5 * gate * jax.nn.sigmoid(gate) * up).astype(jnp.bfloat16)
        return _dot(act, w2_ref[...])

    @pl.when(j == 0)
    def _():
        x = x_ref[...]
        xn = _rms(x, g_ref[...]).astype(jnp.bfloat16)
        xn_ref[...] = xn
        o_ref[...] = x + half_swiglu(xn)

    @pl.when(j != 0)
    def _():
        o_ref[...] += half_swiglu(xn_ref[...])


def _ffn(x, g, w1, w3, w2):
    n, d = x.shape
    f = w1.shape[1]
    tm, tf = FFN_ROW_TILE, FF_TILE
    return pl.pallas_call(
        _ffn_kernel,
        grid=(n // tm, f // tf),
        in_specs=[
            _early_rows(tm, d, n // tm, f // tf // 2),
            pl.BlockSpec((1, d), lambda i, j: (0, 0)),
            pl.BlockSpec((d, tf), lambda i, j: (0, j)),
            pl.BlockSpec((d, tf), lambda i, j: (0, j)),
            pl.BlockSpec((tf, d), lambda i, j: (j, 0)),
        ],
        out_specs=pl.BlockSpec((tm, d), lambda i, j: (i, 0)),
        out_shape=jax.ShapeDtypeStruct((n, d), jnp.float32),
        scratch_shapes=[pltpu.VMEM((tm, d), jnp.bfloat16)],
        compiler_params=_params("parallel", "arbitrary"),
        name="ffn",
    )(x, g, w1, w3, w2)


def _proj_kernel(h_ref, g_ref, wa_ref, wc_ref, wb_ref, wq_ref, wk_ref, wv_ref, qg_ref, kg_ref,
                 u_ref, z_ref, b_ref, q_ref, k_ref, v_ref):
    d = ATT_HEAD_DIM

    def project(u):
        z_ref[...] = _dot(u, wc_ref[...]) * _dot(u, wa_ref[...])
        b_ref[...] = _dot(u, wb_ref[...])
        yq = _dot(u, wq_ref[...])
        yk = _dot(u, wk_ref[...])
        for c in range(2):
            cols = slice(c * d, (c + 1) * d)
            q_ref[:, cols] = (_rms(yq[:, cols], qg_ref[...]) * Q_SCALE).astype(jnp.bfloat16)
            k_ref[:, cols] = _rms(yk[:, cols], kg_ref[...]).astype(jnp.bfloat16)
        v_ref[...] = _dot(u, wv_ref[...]).astype(jnp.bfloat16)

    @pl.when(pl.program_id(1) == 0)
    def _():
        u = _rms(h_ref[...], g_ref[...]).astype(jnp.bfloat16)
        u_ref[...] = u
        project(u)

    @pl.when(pl.program_id(1) != 0)
    def _():
        project(u_ref[...])


def _proj(h, g, w_in, q_gain, k_gain):
    n, d = h.shape
    tm, w = PROJ_ROW_TILE, HEAD_W
    nblk = w_in.shape[1] // (6 * w)
    col = pl.BlockSpec((tm, w), lambda i, j: (i, j))
    wspec = lambda p: pl.BlockSpec((d, w), lambda i, j: (0, p * nblk + j))
    gain = pl.BlockSpec((1, ATT_HEAD_DIM), lambda i, j: (0, 0))
    f32 = jax.ShapeDtypeStruct((n, nblk * w), jnp.float32)
    bf16 = jax.ShapeDtypeStruct((n, nblk * w), jnp.bfloat16)
    return pl.pallas_call(
        _proj_kernel,
        grid=(n // tm, nblk),
        in_specs=[
            _early_rows(tm, d, n // tm, nblk // 2),
            pl.BlockSpec((1, d), lambda i, j: (0, 0)),
            wspec(0), wspec(1), wspec(2), wspec(3), wspec(4), wspec(5),
            gain, gain,
        ],
        out_specs=[pl.BlockSpec((tm, d), lambda i, j: (i, 0)), col, col, col, col, col],
        out_shape=[jax.ShapeDtypeStruct((n, d), jnp.bfloat16), f32, f32, bf16, bf16, bf16],
        compiler_params=_params("parallel", "arbitrary"),
        name="proj",
    )(h, g, w_in, w_in, w_in, w_in, w_in, w_in, q_gain, k_gain)


def _bias_tile_kernel(tab_ref, o_ref):
    h = pl.program_id(0)
    t = o_ref.shape[-1]
    offset = (pl.program_id(1) - BIAS_TILE_REACH) * t
    rel = (lax.broadcasted_iota(jnp.int32, (t, t), 1)
           - lax.broadcasted_iota(jnp.int32, (t, t), 0) + offset)
    n = jnp.abs(rel)
    nb = N_BUCKETS // 2
    max_exact = nb // 2
    large = jnp.full((t, t), max_exact, jnp.int32)
    for start in _LOG_BUCKET_STARTS:
        large = large + (n >= start).astype(jnp.int32)
    bucket = jnp.where(rel > 0, nb, 0) + jnp.where(n < max_exact, n, large)
    acc = jnp.zeros((t, t), jnp.float32)
    for bkt in range(N_BUCKETS):
        acc = jnp.where(bucket == bkt, tab_ref[bkt * ATT_HEADS + h], acc)
    o_ref[0, 0] = acc


def _bias_tiles(table_flat, t):
    n_off = 2 * BIAS_TILE_REACH + 1
    return pl.pallas_call(
        _bias_tile_kernel,
        grid=(ATT_HEADS, n_off),
        in_specs=[pl.BlockSpec(memory_space=pltpu.SMEM)],
        out_specs=pl.BlockSpec((1, 1, t, t), lambda h, o: (h, o, 0, 0)),
        out_shape=jax.ShapeDtypeStruct((ATT_HEADS, n_off, t, t), jnp.float32),
        compiler_params=_params("parallel", "parallel"),
        name="bias_tiles",
    )(table_flat)


def _attn_kernel(fixed_ref, lam_ref, q_ref, k_ref, v_ref, bias_ref, sg_ref, o_ref,
                 m_ref, l_ref, acc_ref, *, lam_init):
    qi = pl.program_id(2)
    t = q_ref.shape[0]
    d = ATT_HEAD_DIM

    def bias_tile(kc):
        off = jnp.clip(kc - qi, -BIAS_TILE_REACH, BIAS_TILE_REACH)
        return bias_ref[0, off + BIAS_TILE_REACH]

    def finish(acc, l):
        lam_q1, lam_k1, lam_q2, lam_k2 = (lam_ref[r:r + 1, :] for r in range(4))
        lam = (jnp.exp(jnp.sum(lam_q1 * lam_k1, axis=-1, keepdims=True))
               - jnp.exp(jnp.sum(lam_q2 * lam_k2, axis=-1, keepdims=True)) + lam_init)
        o = acc[0] / l[0] - lam * (acc[1] / l[1])
        o_ref[...] = (_rms(o, sg_ref[...]) * (1.0 - lam_init)).astype(jnp.bfloat16)

    def fixed_shift_tile():
        acc = [jnp.zeros((t, HEAD_W), jnp.float32) for _ in range(2)]
        lp = [jnp.zeros((t, LANES), jnp.float32) for _ in range(2)]
        for kc in range(k_ref.shape[0] // t):
            bias = bias_tile(kc)
            v = v_ref[kc * t:(kc + 1) * t, :]
            for c in range(2):
                cols = slice(c * d, (c + 1) * d)
                p = jnp.exp2(_dot_nt(q_ref[:, cols], k_ref[kc * t:(kc + 1) * t, cols]) + bias)
                for s in range(t // LANES):
                    lp[c] = lp[c] + p[:, s * LANES:(s + 1) * LANES]
                acc[c] = acc[c] + _dot(p.astype(jnp.bfloat16), v)
        finish(acc, [jnp.sum(x, axis=-1, keepdims=True) for x in lp])

    def online_step(kc):
        rows = pl.ds(pl.multiple_of(kc * t, t), t)
        bias = bias_tile(kc)
        v = v_ref[rows, :]
        for c in range(2):
            cols = slice(c * d, (c + 1) * d)
            x = _dot_nt(q_ref[:, cols], k_ref[rows, cols]) + bias
            m_old = m_ref[c]
            m_new = jnp.maximum(m_old, jnp.max(x, axis=-1, keepdims=True))
            alpha = jnp.exp2(m_old - m_new)
            p = jnp.exp2(x - m_new)
            l_ref[c] = alpha * l_ref[c] + jnp.sum(p, axis=-1, keepdims=True)
            acc_ref[c] = alpha * acc_ref[c] + _dot(p.astype(jnp.bfloat16), v)
            m_ref[c] = m_new

    def online_tile():
        m_ref[...] = jnp.full_like(m_ref, NEG_INIT)
        l_ref[...] = jnp.zeros_like(l_ref)
        acc_ref[...] = jnp.zeros_like(acc_ref)

        def body(kc, carry):
            online_step(kc)
            return carry

        lax.fori_loop(0, k_ref.shape[0] // t, body, 0)
        finish([acc_ref[0], acc_ref[1]], [l_ref[0], l_ref[1]])

    pl.when(fixed_ref[0] == 1)(fixed_shift_tile)
    pl.when(fixed_ref[0] != 1)(online_tile)


def _attn(fixed_flag, lam_rows, q, k, v, bias_tiles, sub_gain, batch, seq, lam_init):
    n, width = q.shape
    t = ATT_TILE
    assert FAR_DISTANCE <= (BIAS_TILE_REACH - 1) * t + 1 and seq % t == 0
    nq = seq // t
    w = HEAD_W
    return pl.pallas_call(
        functools.partial(_attn_kernel, lam_init=lam_init),
        grid=(batch, ATT_HEADS, nq),
        in_specs=[
            pl.BlockSpec(memory_space=pltpu.SMEM),
            pl.BlockSpec((4, ATT_HEAD_DIM), lambda b, h, i: (0, 0)),
            pl.BlockSpec((t, w), lambda b, h, i: (b * nq + i, h)),
            pl.BlockSpec((seq, w), lambda b, h, i: (b, h)),
            pl.BlockSpec((seq, w), lambda b, h, i: (b, h)),
            pl.BlockSpec((1, 2 * BIAS_TILE_REACH + 1, t, t), lambda b, h, i: (h, 0, 0, 0)),
            pl.BlockSpec((1, w), lambda b, h, i: (0, 0)),
        ],
        out_specs=pl.BlockSpec((t, w), lambda b, h, i: (b * nq + i, h)),
        out_shape=jax.ShapeDtypeStruct((n, width), jnp.bfloat16),
        scratch_shapes=[
            pltpu.VMEM((2, t, 1), jnp.float32),
            pltpu.VMEM((2, t, 1), jnp.float32),
            pltpu.VMEM((2, t, w), jnp.float32),
        ],
        compiler_params=_params("parallel", "parallel", "arbitrary"),
        name="attn",
    )(fixed_flag, lam_rows, q, k, v, bias_tiles, sub_gain)


def _shifted_bias_table(rel_bias, q_gain, k_gain):
    table = rel_bias.astype(jnp.float32) * LOG2E
    dot_bound = (Q_SCALE * ATT_HEAD_DIM * BF16_ROUNDING_MARGIN
                 * jnp.max(jnp.abs(q_gain)) * jnp.max(jnp.abs(k_gain)))
    span = 2.0 * dot_bound + (jnp.max(table) - jnp.min(table))
    fixed = (span <= FIXED_SHIFT_MAX_SPAN).astype(jnp.int32).reshape(1)
    shift = jnp.where(fixed[0] == 1, dot_bound + jnp.max(table), 0.0)
    return (table - shift).reshape(-1), fixed


def _merge_kernel(h_ref, u_ref, z_ref, zp_ref, zn_ref, b_ref, o_ref, cw_ref,
                  wga_ref, wgb_ref, wa_ref, wb_ref, wo_ref, out_ref, a_ref, *, tiles_per_seq):
    i = pl.program_id(0)
    j = pl.program_id(1)
    tm = h_ref.shape[0]
    halo = zp_ref.shape[0]

    def mix(a):
        u = u_ref[...]
        g_a = jax.nn.sigmoid(_dot(u, wga_ref[...]))
        g_b = jax.nn.sigmoid(_dot(u, wgb_ref[...]))
        y_a = _dot(a, wa_ref[...])
        y_b = _dot(o_ref[...], wb_ref[...])
        return _dot((g_a * y_a + g_b * y_b).astype(jnp.bfloat16), wo_ref[...])

    @pl.when(j == 0)
    def _():
        z = z_ref[...]
        pos = i % tiles_per_seq
        before = jnp.where(pos == 0, 0.0, zp_ref[halo - 1:halo, :])
        after = jnp.where(pos == tiles_per_seq - 1, 0.0, zn_ref[0:1, :])
        row = lax.broadcasted_iota(jnp.int32, z.shape, 0)
        z_prev = jnp.where(row == 0, before, pltpu.roll(z, 1, 0))
        z_next = jnp.where(row == tm - 1, after, pltpu.roll(z, tm - 1, 0))
        conv = z_prev * cw_ref[0:1, :] + z * cw_ref[1:2, :] + z_next * cw_ref[2:3, :]
        a = (b_ref[...] * conv).astype(jnp.bfloat16)
        a_ref[...] = a
        out_ref[...] = h_ref[...] + mix(a)

    @pl.when(j != 0)
    def _():
        out_ref[...] += mix(a_ref[...])


def _merge(h, u, z, b, o, conv_w, wg, wa, wb, wo, seq):
    n, d = h.shape
    cw = z.shape[1]
    tm, tn = ROW_TILE, MERGE_COL_TILE
    halo = 8
    nj = d // tn
    hb = tm // halo
    last_hb = n // halo - 1
    n_tiles = n // tm
    rows = lambda width: pl.BlockSpec((tm, width), lambda i, j: (i, 0))
    z_tile = lambda i, j: jnp.where(j < 1, i, jnp.minimum(i + 1, n_tiles - 1))
    return pl.pallas_call(
        functools.partial(_merge_kernel, tiles_per_seq=seq // tm),
        grid=(n // tm, nj),
        in_specs=[
            _early_rows(tm, d, n_tiles, 3),
            rows(d),
            _early_rows(tm, cw, n_tiles, 1),
            pl.BlockSpec((halo, cw), lambda i, j: (jnp.maximum(z_tile(i, j) * hb - 1, 0), 0)),
            pl.BlockSpec((halo, cw), lambda i, j: (jnp.minimum((z_tile(i, j) + 1) * hb, last_hb), 0)),
            _early_rows(tm, cw, n_tiles, 2),
            rows(o.shape[1]),
            pl.BlockSpec((3, cw), lambda i, j: (0, 0)),
            pl.BlockSpec((d, tn), lambda i, j: (0, j)),
            pl.BlockSpec((d, tn), lambda i, j: (0, nj + j)),
            pl.BlockSpec((cw, tn), lambda i, j: (0, j)),
            pl.BlockSpec((o.shape[1], tn), lambda i, j: (0, j)),
            pl.BlockSpec((tn, d), lambda i, j: (j, 0)),
        ],
        out_specs=rows(d),
        out_shape=jax.ShapeDtypeStruct((n, d), jnp.float32),
        scratch_shapes=[pltpu.VMEM((tm, cw), jnp.bfloat16)],
        compiler_params=_params("parallel", "arbitrary"),
        name="merge",
    )(h, u, z, z, z, b, o, conv_w, wg, wg, wa, wb, wo)


def _ple_kernel(h_ref, g_ref, p_ref, wg_ref, wp_ref, o_ref):
    h = h_ref[...]
    gate = jax.nn.sigmoid(_dot(_rms(h, g_ref[...]).astype(jnp.bfloat16), wg_ref[...]))
    o_ref[...] = h + gate * _dot(p_ref[...].astype(jnp.bfloat16), wp_ref[...])


def _ple(h, g, p, wg, wp):
    n, d = h.shape
    e = p.shape[1]
    tm = ROW_TILE
    return pl.pallas_call(
        _ple_kernel,
        grid=(n // tm,),
        in_specs=[
            pl.BlockSpec((tm, d), lambda i: (i, 0)),
            pl.BlockSpec((1, d), lambda i: (0, 0)),
            pl.BlockSpec((tm, e), lambda i: (i, 0)),
            pl.BlockSpec((d, d), lambda i: (0, 0)),
            pl.BlockSpec((e, d), lambda i: (0, 0)),
        ],
        out_specs=pl.BlockSpec((tm, d), lambda i: (i, 0)),
        out_shape=jax.ShapeDtypeStruct((n, d), jnp.float32),
        compiler_params=_params("parallel"),
        name="ple",
    )(h, g, p, wg, wp)


def _bf16(w):
    return w.astype(jnp.bfloat16)


def kernel(x, p, ffn1_norm, ffn1_w1, ffn1_w3, ffn1_w2, mix_norm, w_in, conv_w, q_norm, k_norm,
           lam_q1, lam_k1, lam_q2, lam_k2, sub_norm, rel_bias, w_branch_a, w_branch_b, w_gate,
           w_out, ffn2_norm, ffn2_w1, ffn2_w3, ffn2_w2, ple_norm, w_ple_gate, w_ple_proj):
    batch, seq, d = x.shape
    n = batch * seq
    depth = ffn1_w1.shape[0]
    h = x.reshape(n, d)

    for l in range(depth):
        row = lambda a: a[l].reshape(1, -1)
        h = _ffn(h, row(ffn1_norm), _bf16(ffn1_w1[l]), _bf16(ffn1_w3[l]), _bf16(ffn1_w2[l]))

        u, z, b_gate, q, k, v = _proj(h, row(mix_norm), _bf16(w_in[l]), row(q_norm), row(k_norm))

        lam_init = 0.8 - 0.6 * math.exp(-0.3 * l)
        lam_rows = jnp.stack([lam_q1[l], lam_k1[l], lam_q2[l], lam_k2[l]])
        table_flat, fixed_flag = _shifted_bias_table(rel_bias, q_norm[l], k_norm[l])
        bias_tiles = _bias_tiles(table_flat, ATT_TILE)
        o = _attn(fixed_flag, lam_rows, q, k, v, bias_tiles, row(sub_norm), batch, seq, lam_init)

        h = _merge(h, u, z, b_gate, o, conv_w[l], _bf16(w_gate[l]),
                   _bf16(w_branch_a[l]), _bf16(w_branch_b[l]), _bf16(w_out[l]), seq)

        h = _ffn(h, row(ffn2_norm), _bf16(ffn2_w1[l]), _bf16(ffn2_w3[l]), _bf16(ffn2_w2[l]))
        h = _ple(h, row(ple_norm), p[l].reshape(n, -1), _bf16(w_ple_gate[l]), _bf16(w_ple_proj[l]))
    return h.reshape(batch, seq, d)
```

```python
import functools
import math

import jax
import jax.numpy as jnp
from jax import lax
from jax.experimental import pallas as pl
from jax.experimental.pallas import tpu as pltpu

EPS = 1e-6
N_BUCKETS = 32
ATT_HEADS = 4
ATT_HEAD_DIM = 128
HEAD_W = 2 * ATT_HEAD_DIM
LANES = 128
NEG_INIT = -1e30
LOG2E = 1.0 / math.log(2.0)
Q_SCALE = LOG2E * ATT_HEAD_DIM ** -0.5

FIXED_SHIFT_MAX_SPAN = 100.0
BF16_ROUNDING_MARGIN = 1.01

VMEM_LIMIT_BYTES = 56 * 1024 * 1024

ROW_TILE = 512
PROJ_ROW_TILE = 1024
FFN_ROW_TILE = 1024
FF_TILE = 512
ATT_TILE = 512
ATT_TILES_PER_STEP = 2
ATT_KEYS_PER_DOT = 256
BIAS_TILE_REACH = 2
MERGE_COL_TILE = 512

_LOG_BUCKET_STARTS = (12, 16, 23, 32, 46, 64, 91)
FAR_DISTANCE = _LOG_BUCKET_STARTS[-1]


def _dot(a, b):
    return jnp.dot(a, b, preferred_element_type=jnp.float32)


def _dot_nt(a, b):
    return lax.dot_general(a, b, (((1,), (1,)), ((), ())),
                           preferred_element_type=jnp.float32)


def _rms(x, g):
    ms = jnp.mean(x * x, axis=-1, keepdims=True)
    return x * lax.rsqrt(ms + EPS) * g


def _early_rows(tile, width, n_tiles, switch_step):
    def index(i, j):
        return (jnp.where(j < switch_step, i, jnp.minimum(i + 1, n_tiles - 1)), 0)
    return pl.BlockSpec((tile, width), index)


def _params(*sem):
    return pltpu.CompilerParams(dimension_semantics=sem,
                                vmem_limit_bytes=VMEM_LIMIT_BYTES)


def _ffn_kernel(x_ref, g_ref, w1_ref, w3_ref, w2_ref, o_ref, xn_ref):
    j = pl.program_id(1)

    def half_swiglu(xn):
        gate = _dot(xn, w1_ref[...])
        up = _dot(xn, w3_ref[...])
        act = (0.5 * gate * jax.nn.sigmoid(gate) * up).astype(jnp.bfloat16)
        return _dot(act, w2_ref[...])

    @pl.when(j == 0)
    def _():
        x = x_ref[...]
        xn = _rms(x, g_ref[...]).astype(jnp.bfloat16)
        xn_ref[...] = xn
        o_ref[...] = x + half_swiglu(xn)

    @pl.when(j != 0)
    def _():
        o_ref[...] += half_swiglu(xn_ref[...])


def _ffn(x, g, w1, w3, w2):
    n, d = x.shape
    f = w1.shape[1]
    tm, tf = FFN_ROW_TILE, FF_TILE
    return pl.pallas_call(
        _ffn_kernel,
        grid=(n // tm, f // tf),
        in_specs=[
            _early_rows(tm, d, n // tm, f // tf // 2),
            pl.BlockSpec((1, d), lambda i, j: (0, 0)),
            pl.BlockSpec((d, tf), lambda i, j: (0, j)),
            pl.BlockSpec((d, tf), lambda i, j: (0, j)),
            pl.BlockSpec((tf, d), lambda i, j: (j, 0)),
        ],
        out_specs=pl.BlockSpec((tm, d), lambda i, j: (i, 0)),
        out_shape=jax.ShapeDtypeStruct((n, d), jnp.float32),
        scratch_shapes=[pltpu.VMEM((tm, d), jnp.bfloat16)],
        compiler_params=_params("parallel", "arbitrary"),
        name="ffn",
    )(x, g, w1, w3, w2)


def _proj_kernel(h_ref, g_ref, wa_ref, wc_ref, wb_ref, wq_ref, wk_ref, wv_ref, qg_ref, kg_ref,
                 u_ref, z_ref, b_ref, q_ref, k_ref, v_ref):
    d = ATT_HEAD_DIM

    def project(u):
        z_ref[...] = _dot(u, wc_ref[...]) * _dot(u, wa_ref[...])
        b_ref[...] = _dot(u, wb_ref[...])
        yq = _dot(u, wq_ref[...])
        yk = _dot(u, wk_ref[...])
        for c in range(2):
            cols = slice(c * d, (c + 1) * d)
            q_ref[:, cols] = (_rms(yq[:, cols], qg_ref[...]) * Q_SCALE).astype(jnp.bfloat16)
            k_ref[:, cols] = _rms(yk[:, cols], kg_ref[...]).astype(jnp.bfloat16)
        v_ref[...] = _dot(u, wv_ref[...]).astype(jnp.bfloat16)

    @pl.when(pl.program_id(1) == 0)
    def _():
        u = _rms(h_ref[...], g_ref[...]).astype(jnp.bfloat16)
        u_ref[...] = u
        project(u)

    @pl.when(pl.program_id(1) != 0)
    def _():
        project(u_ref[...])


def _proj(h, g, w_in, q_gain, k_gain):
    n, d = h.shape
    tm, w = PROJ_ROW_TILE, HEAD_W
    nblk = w_in.shape[1] // (6 * w)
    col = pl.BlockSpec((tm, w), lambda i, j: (i, j))
    wspec = lambda p: pl.BlockSpec((d, w), lambda i, j: (0, p * nblk + j))
    gain = pl.BlockSpec((1, ATT_HEAD_DIM), lambda i, j: (0, 0))
    f32 = jax.ShapeDtypeStruct((n, nblk * w), jnp.float32)
    bf16 = jax.ShapeDtypeStruct((n, nblk * w), jnp.bfloat16)
    return pl.pallas_call(
        _proj_kernel,
        grid=(n // tm, nblk),
        in_specs=[
            _early_rows(tm, d, n // tm, nblk // 2),
            pl.BlockSpec((1, d), lambda i, j: (0, 0)),
            wspec(0), wspec(1), wspec(2), wspec(3), wspec(4), wspec(5),
            gain, gain,
        ],
        out_specs=[pl.BlockSpec((tm, d), lambda i, j: (i, 0)), col, col, col, col, col],
        out_shape=[jax.ShapeDtypeStruct((n, d), jnp.bfloat16), f32, f32, bf16, bf16, bf16],
        compiler_params=_params("parallel", "arbitrary"),
        name="proj",
    )(h, g, w_in, w_in, w_in, w_in, w_in, w_in, q_gain, k_gain)


def _bias_tile_kernel(tab_ref, o_ref):
    h = pl.program_id(0)
    t = o_ref.shape[-1]
    offset = (pl.program_id(1) - BIAS_TILE_REACH) * t
    rel = (lax.broadcasted_iota(jnp.int32, (t, t), 1)
           - lax.broadcasted_iota(jnp.int32, (t, t), 0) + offset)
    n = jnp.abs(rel)
    nb = N_BUCKETS // 2
    max_exact = nb // 2
    large = jnp.full((t, t), max_exact, jnp.int32)
    for start in _LOG_BUCKET_STARTS:
        large = large + (n >= start).astype(jnp.int32)
    bucket = jnp.where(rel > 0, nb, 0) + jnp.where(n < max_exact, n, large)
    acc = jnp.zeros((t, t), jnp.float32)
    for bkt in range(N_BUCKETS):
        acc = jnp.where(bucket == bkt, tab_ref[bkt * ATT_HEADS + h], acc)
    o_ref[0, 0] = acc


def _bias_tiles(table_flat, t):
    n_off = 2 * BIAS_TILE_REACH + 1
    return pl.pallas_call(
        _bias_tile_kernel,
        grid=(ATT_HEADS, n_off),
        in_specs=[pl.BlockSpec(memory_space=pltpu.SMEM)],
        out_specs=pl.BlockSpec((1, 1, t, t), lambda h, o: (h, o, 0, 0)),
        out_shape=jax.ShapeDtypeStruct((ATT_HEADS, n_off, t, t), jnp.float32),
        compiler_params=_params("parallel", "parallel"),
        name="bias_tiles",
    )(table_flat)


def _attn_kernel(fixed_ref, lam_ref, q_ref, k_ref, v_ref, bias_ref, sg_ref, o_ref,
                 m_ref, l_ref, acc_ref, *, lam_init):
    t = bias_ref.shape[-1]
    tiles = q_ref.shape[0] // t
    n_chunks = k_ref.shape[0] // t
    d = ATT_HEAD_DIM

    def bias_tile(qt, kc):
        off = jnp.clip(kc - qt, -BIAS_TILE_REACH, BIAS_TILE_REACH)
        return bias_ref[0, off + BIAS_TILE_REACH]

    def finish(q_rows, acc, l):
        lam_q1, lam_k1, lam_q2, lam_k2 = (lam_ref[r:r + 1, :] for r in range(4))
        lam = (jnp.exp(jnp.sum(lam_q1 * lam_k1, axis=-1, keepdims=True))
               - jnp.exp(jnp.sum(lam_q2 * lam_k2, axis=-1, keepdims=True)) + lam_init)
        o = acc[0] / l[0] - lam * (acc[1] / l[1])
        o_ref[q_rows, :] = (_rms(o, sg_ref[...]) * (1.0 - lam_init)).astype(jnp.bfloat16)

    def fixed_shift_tile(sub):
        qt = pl.program_id(2) * tiles + sub
        q_rows = slice(sub * t, (sub + 1) * t)
        acc = [jnp.zeros((t, HEAD_W), jnp.float32) for _ in range(2)]
        lp = [jnp.zeros((t, LANES), jnp.float32) for _ in range(2)]
        kd = ATT_KEYS_PER_DOT
        for kc in range(n_chunks):
            tile = bias_tile(qt, kc)
            for part in range(t // kd):
                rows = slice(kc * t + part * kd, kc * t + (part + 1) * kd)
                bias = tile[:, part * kd:(part + 1) * kd]
                v = v_ref[rows, :]
                for c in range(2):
                    cols = slice(c * d, (c + 1) * d)
                    p = jnp.exp2(_dot_nt(q_ref[q_rows, cols], k_ref[rows, cols]) + bias)
                    for s in range(kd // LANES):
                        lp[c] = lp[c] + p[:, s * LANES:(s + 1) * LANES]
                    acc[c] = acc[c] + _dot(p.astype(jnp.bfloat16), v)
        finish(q_rows, acc, [jnp.sum(x, axis=-1, keepdims=True) for x in lp])

    def fixed_shift_block():
        for sub in range(tiles):
            fixed_shift_tile(sub)

    def online_step(sub, qt, kc):
        rows = pl.ds(pl.multiple_of(kc * t, t), t)
        bias = bias_tile(qt, kc)
        v = v_ref[rows, :]
        for c in range(2):
            cols = slice(c * d, (c + 1) * d)
            x = _dot_nt(q_ref[sub * t:(sub + 1) * t, cols], k_ref[rows, cols]) + bias
            m_old = m_ref[c]
            m_new = jnp.maximum(m_old, jnp.max(x, axis=-1, keepdims=True))
            alpha = jnp.exp2(m_old - m_new)
            p = jnp.exp2(x - m_new)
            l_ref[c] = alpha * l_ref[c] + jnp.sum(p, axis=-1, keepdims=True)
            acc_ref[c] = alpha * acc_ref[c] + _dot(p.astype(jnp.bfloat16), v)
            m_ref[c] = m_new

    def online_block():
        for sub in range(tiles):
            qt = pl.program_id(2) * tiles + sub
            m_ref[...] = jnp.full_like(m_ref, NEG_INIT)
            l_ref[...] = jnp.zeros_like(l_ref)
            acc_ref[...] = jnp.zeros_like(acc_ref)

            def body(kc, carry, sub=sub, qt=qt):
                online_step(sub, qt, kc)
                return carry

            lax.fori_loop(0, n_chunks, body, 0)
            finish(slice(sub * t, (sub + 1) * t), [acc_ref[0], acc_ref[1]], [l_ref[0], l_ref[1]])

    pl.when(fixed_ref[0] == 1)(fixed_shift_block)
    pl.when(fixed_ref[0] != 1)(online_block)


def _attn(fixed_flag, lam_rows, q, k, v, bias_tiles, sub_gain, batch, seq, lam_init):
    n, width = q.shape
    t = ATT_TILE
    rows = ATT_TILES_PER_STEP * t
    assert FAR_DISTANCE <= (BIAS_TILE_REACH - 1) * t + 1 and seq % rows == 0
    nq = seq // rows
    w = HEAD_W
    return pl.pallas_call(
        functools.partial(_attn_kernel, lam_init=lam_init),
        grid=(batch, ATT_HEADS, nq),
        in_specs=[
            pl.BlockSpec(memory_space=pltpu.SMEM),
            pl.BlockSpec((4, ATT_HEAD_DIM), lambda b, h, i: (0, 0)),
            pl.BlockSpec((rows, w), lambda b, h, i: (b * nq + i, h)),
            pl.BlockSpec((seq, w), lambda b, h, i: (b, h)),
            pl.BlockSpec((seq, w), lambda b, h, i: (b, h)),
            pl.BlockSpec((1, 2 * BIAS_TILE_REACH + 1, t, t), lambda b, h, i: (h, 0, 0, 0)),
            pl.BlockSpec((1, w), lambda b, h, i: (0, 0)),
        ],
        out_specs=pl.BlockSpec((rows, w), lambda b, h, i: (b * nq + i, h)),
        out_shape=jax.ShapeDtypeStruct((n, width), jnp.bfloat16),
        scratch_shapes=[
            pltpu.VMEM((2, t, 1), jnp.float32),
            pltpu.VMEM((2, t, 1), jnp.float32),
            pltpu.VMEM((2, t, w), jnp.float32),
        ],
        compiler_params=_params("parallel", "parallel", "arbitrary"),
        name="attn",
    )(fixed_flag, lam_rows, q, k, v, bias_tiles, sub_gain)


def _shifted_bias_table(rel_bias, q_gain, k_gain):
    table = rel_bias.astype(jnp.float32) * LOG2E
    dot_bound = (Q_SCALE * ATT_HEAD_DIM * BF16_ROUNDING_MARGIN
                 * jnp.max(jnp.abs(q_gain)) * jnp.max(jnp.abs(k_gain)))
    span = 2.0 * dot_bound + (jnp.max(table) - jnp.min(table))
    fixed = (span <= FIXED_SHIFT_MAX_SPAN).astype(jnp.int32).reshape(1)
    shift = jnp.where(fixed[0] == 1, dot_bound + jnp.max(table), 0.0)
    return (table - shift).reshape(-1), fixed


def _merge_kernel(h_ref, u_ref, z_ref, zp_ref, zn_ref, b_ref, o_ref, cw_ref,
                  wga_ref, wgb_ref, wa_ref, wb_ref, wo_ref, out_ref, a_ref, *, tiles_per_seq):
    i = pl.program_id(0)
    j = pl.program_id(1)
    tm = h_ref.shape[0]
    halo = zp_ref.shape[0]

    def mix(a):
        u = u_ref[...]
        g_a = jax.nn.sigmoid(_dot(u, wga_ref[...]))
        g_b = jax.nn.sigmoid(_dot(u, wgb_ref[...]))
        y_a = _dot(a, wa_ref[...])
        y_b = _dot(o_ref[...], wb_ref[...])
        return _dot((g_a * y_a + g_b * y_b).astype(jnp.bfloat16), wo_ref[...])

    @pl.when(j == 0)
    def _():
        z = z_ref[...]
        pos = i % tiles_per_seq
        before = jnp.where(pos == 0, 0.0, zp_ref[halo - 1:halo, :])
        after = jnp.where(pos == tiles_per_seq - 1, 0.0, zn_ref[0:1, :])
        row = lax.broadcasted_iota(jnp.int32, z.shape, 0)
        z_prev = jnp.where(row == 0, before, pltpu.roll(z, 1, 0))
        z_next = jnp.where(row == tm - 1, after, pltpu.roll(z, tm - 1, 0))
        conv = z_prev * cw_ref[0:1, :] + z * cw_ref[1:2, :] + z_next * cw_ref[2:3, :]
        a = (b_ref[...] * conv).astype(jnp.bfloat16)
        a_ref[...] = a
        out_ref[...] = h_ref[...] + mix(a)

    @pl.when(j != 0)
    def _():
        out_ref[...] += mix(a_ref[...])


def _merge(h, u, z, b, o, conv_w, wg, wa, wb, wo, seq):
    n, d = h.shape
    cw = z.shape[1]
    tm, tn = ROW_TILE, MERGE_COL_TILE
    halo = 8
    nj = d // tn
    hb = tm // halo
    last_hb = n // halo - 1
    n_tiles = n // tm
    rows = lambda width: pl.BlockSpec((tm, width), lambda i, j: (i, 0))
    z_tile = lambda i, j: jnp.where(j < 1, i, jnp.minimum(i + 1, n_tiles - 1))
    return pl.pallas_call(
        functools.partial(_merge_kernel, tiles_per_seq=seq // tm),
        grid=(n // tm, nj),
        in_specs=[
            _early_rows(tm, d, n_tiles, 3),
            rows(d),
            _early_rows(tm, cw, n_tiles, 1),
            pl.BlockSpec((halo, cw), lambda i, j: (jnp.maximum(z_tile(i, j) * hb - 1, 0), 0)),
            pl.BlockSpec((halo, cw), lambda i, j: (jnp.minimum((z_tile(i, j) + 1) * hb, last_hb), 0)),
            _early_rows(tm, cw, n_tiles, 2),
            rows(o.shape[1]),
            pl.BlockSpec((3, cw), lambda i, j: (0, 0)),
            pl.BlockSpec((d, tn), lambda i, j: (0, j)),
            pl.BlockSpec((d, tn), lambda i, j: (0, nj + j)),
            pl.BlockSpec((cw, tn), lambda i, j: (0, j)),
            pl.BlockSpec((o.shape[1], tn), lambda i, j: (0, j)),
            pl.BlockSpec((tn, d), lambda i, j: (j, 0)),
        ],
        out_specs=rows(d),
        out_shape=jax.ShapeDtypeStruct((n, d), jnp.float32),
        scratch_shapes=[pltpu.VMEM((tm, cw), jnp.bfloat16)],
        compiler_params=_params("parallel", "arbitrary"),
        name="merge",
    )(h, u, z, z, z, b, o, conv_w, wg, wg, wa, wb, wo)


def _ple_kernel(h_ref, g_ref, p_ref, wg_ref, wp_ref, o_ref):
    h = h_ref[...]
    gate = jax.nn.sigmoid(_dot(_rms(h, g_ref[...]).astype(jnp.bfloat16), wg_ref[...]))
    o_ref[...] = h + gate * _dot(p_ref[...].astype(jnp.bfloat16), wp_ref[...])


def _ple(h, g, p, wg, wp):
    n, d = h.shape
    e = p.shape[1]
    tm = ROW_TILE
    return pl.pallas_call(
        _ple_kernel,
        grid=(n // tm,),
        in_specs=[
            pl.BlockSpec((tm, d), lambda i: (i, 0)),
            pl.BlockSpec((1, d), lambda i: (0, 0)),
            pl.BlockSpec((tm, e), lambda i: (i, 0)),
            pl.BlockSpec((d, d), lambda i: (0, 0)),
            pl.BlockSpec((e, d), lambda i: (0, 0)),
        ],
        out_specs=pl.BlockSpec((tm, d), lambda i: (i, 0)),
        out_shape=jax.ShapeDtypeStruct((n, d), jnp.float32),
        compiler_params=_params("parallel"),
        name="ple",
    )(h, g, p, wg, wp)


def _bf16(w):
    return w.astype(jnp.bfloat16)


def kernel(x, p, ffn1_norm, ffn1_w1, ffn1_w3, ffn1_w2, mix_norm, w_in, conv_w, q_norm, k_norm,
           lam_q1, lam_k1, lam_q2, lam_k2, sub_norm, rel_bias, w_branch_a, w_branch_b, w_gate,
           w_out, ffn2_norm, ffn2_w1, ffn2_w3, ffn2_w2, ple_norm, w_ple_gate, w_ple_proj):
    batch, seq, d = x.shape
    n = batch * seq
    depth = ffn1_w1.shape[0]
    h = x.reshape(n, d)

    for l in range(depth):
        row = lambda a: a[l].reshape(1, -1)
        h = _ffn(h, row(ffn1_norm), _bf16(ffn1_w1[l]), _bf16(ffn1_w3[l]), _bf16(ffn1_w2[l]))

        u, z, b_gate, q, k, v = _proj(h, row(mix_norm), _bf16(w_in[l]), row(q_norm), row(k_norm))

        lam_init = 0.8 - 0.6 * math.exp(-0.3 * l)
        lam_rows = jnp.stack([lam_q1[l], lam_k1[l], lam_q2[l], lam_k2[l]])
        table_flat, fixed_flag = _shifted_bias_table(rel_bias, q_norm[l], k_norm[l])
        bias_tiles = _bias_tiles(table_flat, ATT_TILE)
        o = _attn(fixed_flag, lam_rows, q, k, v, bias_tiles, row(sub_norm), batch, seq, lam_init)

        h = _merge(h, u, z, b_gate, o, conv_w[l], _bf16(w_gate[l]),
                   _bf16(w_branch_a[l]), _bf16(w_branch_b[l]), _bf16(w_out[l]), seq)

        h = _ffn(h, row(ffn2_norm), _bf16(ffn2_w1[l]), _bf16(ffn2_w3[l]), _bf16(ffn2_w2[l]))
        h = _ple(h, row(ple_norm), p[l].reshape(n, -1), _bf16(w_ple_gate[l]), _bf16(w_ple_proj[l]))
    return h.reshape(batch, seq, d)
```

```python
import functools
import math

import jax
import jax.numpy as jnp
from jax import lax
from jax.experimental import pallas as pl
from jax.experimental.pallas import tpu as pltpu

EPS = 1e-6
N_BUCKETS = 32
ATT_HEADS = 4
ATT_HEAD_DIM = 128
HEAD_W = 2 * ATT_HEAD_DIM
LANES = 128
NEG_INIT = -1e30
LOG2E = 1.0 / math.log(2.0)
Q_SCALE = LOG2E * ATT_HEAD_DIM ** -0.5

FIXED_SHIFT_MAX_SPAN = 100.0
BF16_ROUNDING_MARGIN = 1.01

VMEM_LIMIT_BYTES = 56 * 1024 * 1024

ROW_TILE = 512
MERGE_ROW_TILE = 256
GATE_ROW_TILE = 1024
GATE_COL_PIECE = 512
PROJ_ROW_TILE = 1024
FFN_ROW_TILE = 1024
FF_TILE = 512
ATT_TILE = 512
ATT_TILES_PER_STEP = 2
ATT_KEYS_PER_DOT = 256
BIAS_TILE_REACH = 2

_LOG_BUCKET_STARTS = (12, 16, 23, 32, 46, 64, 91)
FAR_DISTANCE = _LOG_BUCKET_STARTS[-1]


def _dot(a, b):
    return jnp.dot(a, b, preferred_element_type=jnp.float32)


def _dot_nt(a, b):
    return lax.dot_general(a, b, (((1,), (1,)), ((), ())),
                           preferred_element_type=jnp.float32)


def _rms(x, g):
    ms = jnp.mean(x * x, axis=-1, keepdims=True)
    return x * lax.rsqrt(ms + EPS) * g


def _early_rows(tile, width, n_tiles, switch_step):
    def index(i, j):
        return (jnp.where(j < switch_step, i, jnp.minimum(i + 1, n_tiles - 1)), 0)
    return pl.BlockSpec((tile, width), index)


def _params(*sem):
    return pltpu.CompilerParams(dimension_semantics=sem,
                                vmem_limit_bytes=VMEM_LIMIT_BYTES)


def _ffn_kernel(x_ref, g_ref, w1_ref, w3_ref, w2_ref, o_ref, xn_ref):
    j = pl.program_id(1)

    def half_swiglu(xn):
        gate = _dot(xn, w1_ref[...])
        up = _dot(xn, w3_ref[...])
        act = (0.5 * gate * jax.nn.sigmoid(gate) * up).astype(jnp.bfloat16)
        return _dot(act, w2_ref[...])

    @pl.when(j == 0)
    def _():
        x = x_ref[...]
        xn = _rms(x, g_ref[...]).astype(jnp.bfloat16)
        xn_ref[...] = xn
        o_ref[...] = x + half_swiglu(xn)

    @pl.when(j != 0)
    def _():
        o_ref[...] += half_swiglu(xn_ref[...])


def _ffn(x, g, w1, w3, w2):
    n, d = x.shape
    f = w1.shape[1]
    tm, tf = FFN_ROW_TILE, FF_TILE
    return pl.pallas_call(
        _ffn_kernel,
        grid=(n // tm, f // tf),
        in_specs=[
            _early_rows(tm, d, n // tm, f // tf // 2),
            pl.BlockSpec((1, d), lambda i, j: (0, 0)),
            pl.BlockSpec((d, tf), lambda i, j: (0, j)),
            pl.BlockSpec((d, tf), lambda i, j: (0, j)),
            pl.BlockSpec((tf, d), lambda i, j: (j, 0)),
        ],
        out_specs=pl.BlockSpec((tm, d), lambda i, j: (i, 0)),
        out_shape=jax.ShapeDtypeStruct((n, d), jnp.float32),
        scratch_shapes=[pltpu.VMEM((tm, d), jnp.bfloat16)],
        compiler_params=_params("parallel", "arbitrary"),
        name="ffn",
    )(x, g, w1, w3, w2)


def _proj_kernel(h_ref, g_ref, wa_ref, wc_ref, wb_ref, wq_ref, wk_ref, wv_ref, qg_ref, kg_ref,
                 u_ref, z_ref, b_ref, q_ref, k_ref, v_ref):
    d = ATT_HEAD_DIM

    def project(u):
        z_ref[...] = _dot(u, wc_ref[...]) * _dot(u, wa_ref[...])
        b_ref[...] = _dot(u, wb_ref[...])
        yq = _dot(u, wq_ref[...])
        yk = _dot(u, wk_ref[...])
        for c in range(2):
            cols = slice(c * d, (c + 1) * d)
            q_ref[:, cols] = (_rms(yq[:, cols], qg_ref[...]) * Q_SCALE).astype(jnp.bfloat16)
            k_ref[:, cols] = _rms(yk[:, cols], kg_ref[...]).astype(jnp.bfloat16)
        v_ref[...] = _dot(u, wv_ref[...]).astype(jnp.bfloat16)

    @pl.when(pl.program_id(1) == 0)
    def _():
        u = _rms(h_ref[...], g_ref[...]).astype(jnp.bfloat16)
        u_ref[...] = u
        project(u)

    @pl.when(pl.program_id(1) != 0)
    def _():
        project(u_ref[...])


def _proj(h, g, w_in, q_gain, k_gain):
    n, d = h.shape
    tm, w = PROJ_ROW_TILE, HEAD_W
    nblk = w_in.shape[1] // (6 * w)
    col = pl.BlockSpec((tm, w), lambda i, j: (i, j))
    wspec = lambda p: pl.BlockSpec((d, w), lambda i, j: (0, p * nblk + j))
    gain = pl.BlockSpec((1, ATT_HEAD_DIM), lambda i, j: (0, 0))
    f32 = jax.ShapeDtypeStruct((n, nblk * w), jnp.float32)
    bf16 = jax.ShapeDtypeStruct((n, nblk * w), jnp.bfloat16)
    return pl.pallas_call(
        _proj_kernel,
        grid=(n // tm, nblk),
        in_specs=[
            _early_rows(tm, d, n // tm, nblk // 2),
            pl.BlockSpec((1, d), lambda i, j: (0, 0)),
            wspec(0), wspec(1), wspec(2), wspec(3), wspec(4), wspec(5),
            gain, gain,
        ],
        out_specs=[pl.BlockSpec((tm, d), lambda i, j: (i, 0)), col, col, col, col, col],
        out_shape=[jax.ShapeDtypeStruct((n, d), jnp.bfloat16), f32, f32, bf16, bf16, bf16],
        compiler_params=_params("parallel", "arbitrary"),
        name="proj",
    )(h, g, w_in, w_in, w_in, w_in, w_in, w_in, q_gain, k_gain)


def _bias_tile_kernel(tab_ref, o_ref):
    h = pl.program_id(0)
    t = o_ref.shape[-1]
    offset = (pl.program_id(1) - BIAS_TILE_REACH) * t
    rel = (lax.broadcasted_iota(jnp.int32, (t, t), 1)
           - lax.broadcasted_iota(jnp.int32, (t, t), 0) + offset)
    n = jnp.abs(rel)
    nb = N_BUCKETS // 2
    max_exact = nb // 2
    large = jnp.full((t, t), max_exact, jnp.int32)
    for start in _LOG_BUCKET_STARTS:
        large = large + (n >= start).astype(jnp.int32)
    bucket = jnp.where(rel > 0, nb, 0) + jnp.where(n < max_exact, n, large)
    acc = jnp.zeros((t, t), jnp.float32)
    for bkt in range(N_BUCKETS):
        acc = jnp.where(bucket == bkt, tab_ref[bkt * ATT_HEADS + h], acc)
    o_ref[0, 0] = acc


def _bias_tiles(table_flat, t):
    n_off = 2 * BIAS_TILE_REACH + 1
    return pl.pallas_call(
        _bias_tile_kernel,
        grid=(ATT_HEADS, n_off),
        in_specs=[pl.BlockSpec(memory_space=pltpu.SMEM)],
        out_specs=pl.BlockSpec((1, 1, t, t), lambda h, o: (h, o, 0, 0)),
        out_shape=jax.ShapeDtypeStruct((ATT_HEADS, n_off, t, t), jnp.float32),
        compiler_params=_params("parallel", "parallel"),
        name="bias_tiles",
    )(table_flat)


def _attn_kernel(fixed_ref, lam_ref, q_ref, k_ref, v_ref, bias_ref, sg_ref, o_ref,
                 m_ref, l_ref, acc_ref, *, lam_init):
    t = bias_ref.shape[-1]
    tiles = q_ref.shape[0] // t
    n_chunks = k_ref.shape[0] // t
    d = ATT_HEAD_DIM

    def bias_tile(qt, kc):
        off = jnp.clip(kc - qt, -BIAS_TILE_REACH, BIAS_TILE_REACH)
        return bias_ref[0, off + BIAS_TILE_REACH]

    def finish(q_rows, acc, l):
        lam_q1, lam_k1, lam_q2, lam_k2 = (lam_ref[r:r + 1, :] for r in range(4))
        lam = (jnp.exp(jnp.sum(lam_q1 * lam_k1, axis=-1, keepdims=True))
               - jnp.exp(jnp.sum(lam_q2 * lam_k2, axis=-1, keepdims=True)) + lam_init)
        o = acc[0] / l[0] - lam * (acc[1] / l[1])
        o_ref[q_rows, :] = (_rms(o, sg_ref[...]) * (1.0 - lam_init)).astype(jnp.bfloat16)

    def fixed_shift_tile(sub):
        qt = pl.program_id(2) * tiles + sub
        q_rows = slice(sub * t, (sub + 1) * t)
        acc = [jnp.zeros((t, HEAD_W), jnp.float32) for _ in range(2)]
        lp = [jnp.zeros((t, LANES), jnp.float32) for _ in range(2)]
        kd = ATT_KEYS_PER_DOT
        for kc in range(n_chunks):
            tile = bias_tile(qt, kc)
            for part in range(t // kd):
                rows = slice(kc * t + part * kd, kc * t + (part + 1) * kd)
                bias = tile[:, part * kd:(part + 1) * kd]
                v = v_ref[rows, :]
                for c in range(2):
                    cols = slice(c * d, (c + 1) * d)
                    p = jnp.exp2(_dot_nt(q_ref[q_rows, cols], k_ref[rows, cols]) + bias)
                    for s in range(kd // LANES):
                        lp[c] = lp[c] + p[:, s * LANES:(s + 1) * LANES]
                    acc[c] = acc[c] + _dot(p.astype(jnp.bfloat16), v)
        finish(q_rows, acc, [jnp.sum(x, axis=-1, keepdims=True) for x in lp])

    def fixed_shift_block():
        for sub in range(tiles):
            fixed_shift_tile(sub)

    def online_step(sub, qt, kc):
        rows = pl.ds(pl.multiple_of(kc * t, t), t)
        bias = bias_tile(qt, kc)
        v = v_ref[rows, :]
        for c in range(2):
            cols = slice(c * d, (c + 1) * d)
            x = _dot_nt(q_ref[sub * t:(sub + 1) * t, cols], k_ref[rows, cols]) + bias
            m_old = m_ref[c]
            m_new = jnp.maximum(m_old, jnp.max(x, axis=-1, keepdims=True))
            alpha = jnp.exp2(m_old - m_new)
            p = jnp.exp2(x - m_new)
            l_ref[c] = alpha * l_ref[c] + jnp.sum(p, axis=-1, keepdims=True)
            acc_ref[c] = alpha * acc_ref[c] + _dot(p.astype(jnp.bfloat16), v)
            m_ref[c] = m_new

    def online_block():
        for sub in range(tiles):
            qt = pl.program_id(2) * tiles + sub
            m_ref[...] = jnp.full_like(m_ref, NEG_INIT)
            l_ref[...] = jnp.zeros_like(l_ref)
            acc_ref[...] = jnp.zeros_like(acc_ref)

            def body(kc, carry, sub=sub, qt=qt):
                online_step(sub, qt, kc)
                return carry

            lax.fori_loop(0, n_chunks, body, 0)
            finish(slice(sub * t, (sub + 1) * t), [acc_ref[0], acc_ref[1]], [l_ref[0], l_ref[1]])

    pl.when(fixed_ref[0] == 1)(fixed_shift_block)
    pl.when(fixed_ref[0] != 1)(online_block)


def _attn(fixed_flag, lam_rows, q, k, v, bias_tiles, sub_gain, batch, seq, lam_init):
    n, width = q.shape
    t = ATT_TILE
    rows = ATT_TILES_PER_STEP * t
    assert FAR_DISTANCE <= (BIAS_TILE_REACH - 1) * t + 1 and seq % rows == 0
    nq = seq // rows
    w = HEAD_W
    return pl.pallas_call(
        functools.partial(_attn_kernel, lam_init=lam_init),
        grid=(batch, ATT_HEADS, nq),
        in_specs=[
            pl.BlockSpec(memory_space=pltpu.SMEM),
            pl.BlockSpec((4, ATT_HEAD_DIM), lambda b, h, i: (0, 0)),
            pl.BlockSpec((rows, w), lambda b, h, i: (b * nq + i, h)),
            pl.BlockSpec((seq, w), lambda b, h, i: (b, h)),
            pl.BlockSpec((seq, w), lambda b, h, i: (b, h)),
            pl.BlockSpec((1, 2 * BIAS_TILE_REACH + 1, t, t), lambda b, h, i: (h, 0, 0, 0)),
            pl.BlockSpec((1, w), lambda b, h, i: (0, 0)),
        ],
        out_specs=pl.BlockSpec((rows, w), lambda b, h, i: (b * nq + i, h)),
        out_shape=jax.ShapeDtypeStruct((n, width), jnp.bfloat16),
        scratch_shapes=[
            pltpu.VMEM((2, t, 1), jnp.float32),
            pltpu.VMEM((2, t, 1), jnp.float32),
            pltpu.VMEM((2, t, w), jnp.float32),
        ],
        compiler_params=_params("parallel", "parallel", "arbitrary"),
        name="attn",
    )(fixed_flag, lam_rows, q, k, v, bias_tiles, sub_gain)


def _shifted_bias_table(rel_bias, q_gain, k_gain):
    table = rel_bias.astype(jnp.float32) * LOG2E
    dot_bound = (Q_SCALE * ATT_HEAD_DIM * BF16_ROUNDING_MARGIN
                 * jnp.max(jnp.abs(q_gain)) * jnp.max(jnp.abs(k_gain)))
    span = 2.0 * dot_bound + (jnp.max(table) - jnp.min(table))
    fixed = (span <= FIXED_SHIFT_MAX_SPAN).astype(jnp.int32).reshape(1)
    shift = jnp.where(fixed[0] == 1, dot_bound + jnp.max(table), 0.0)
    return (table - shift).reshape(-1), fixed


def _gate_kernel(u_ref, wg_ref, g_ref):
    u = u_ref[...]
    piece = GATE_COL_PIECE
    for s in range(g_ref.shape[1] // piece):
        cols = slice(s * piece, (s + 1) * piece)
        g_ref[:, cols] = jax.nn.sigmoid(_dot(u, wg_ref[:, cols])).astype(jnp.bfloat16)


def _gates(u, wg):
    n, d = u.shape
    tm = GATE_ROW_TILE
    return pl.pallas_call(
        _gate_kernel,
        grid=(n // tm,),
        in_specs=[
            pl.BlockSpec((tm, d), lambda i: (i, 0)),
            pl.BlockSpec(wg.shape, lambda i: (0, 0), pipeline_mode=pl.Buffered(1)),
        ],
        out_specs=pl.BlockSpec((tm, wg.shape[1]), lambda i: (i, 0)),
        out_shape=jax.ShapeDtypeStruct((n, wg.shape[1]), jnp.bfloat16),
        compiler_params=_params("parallel"),
        name="gates",
    )(u, wg)


def _merge_kernel(h_ref, ga_ref, gb_ref, z_ref, zp_ref, zn_ref, b_ref, o_ref, cw_ref,
                  wa_ref, wb_ref, wo_ref, out_ref, *, tiles_per_seq):
    tm = h_ref.shape[0]
    halo = zp_ref.shape[0]
    z = z_ref[...]
    pos = pl.program_id(0) % tiles_per_seq
    before = jnp.where(pos == 0, 0.0, zp_ref[halo - 1:halo, :])
    after = jnp.where(pos == tiles_per_seq - 1, 0.0, zn_ref[0:1, :])
    row = lax.broadcasted_iota(jnp.int32, z.shape, 0)
    z_prev = jnp.where(row == 0, before, pltpu.roll(z, 1, 0))
    z_next = jnp.where(row == tm - 1, after, pltpu.roll(z, tm - 1, 0))
    conv = z_prev * cw_ref[0:1, :] + z * cw_ref[1:2, :] + z_next * cw_ref[2:3, :]
    y_a = _dot((b_ref[...] * conv).astype(jnp.bfloat16), wa_ref[...])
    y_b = _dot(o_ref[...], wb_ref[...])
    mixed = ga_ref[...].astype(jnp.float32) * y_a + gb_ref[...].astype(jnp.float32) * y_b
    out_ref[...] = h_ref[...] + _dot(mixed.astype(jnp.bfloat16), wo_ref[...])


def _merge(h, gates, z, b, o, conv_w, wa, wb, wo, seq):
    n, d = h.shape
    cw = z.shape[1]
    tm = MERGE_ROW_TILE
    halo = 8
    hb = tm // halo
    last_hb = n // halo - 1
    rows = lambda width, col=0: pl.BlockSpec((tm, width), lambda i: (i, col))
    resident = lambda w: pl.BlockSpec(w.shape, lambda i: (0, 0), pipeline_mode=pl.Buffered(1))
    return pl.pallas_call(
        functools.partial(_merge_kernel, tiles_per_seq=seq // tm),
        grid=(n // tm,),
        in_specs=[
            rows(d),
            rows(d, 0),
            rows(d, 1),
            rows(cw),
            pl.BlockSpec((halo, cw), lambda i: (jnp.maximum(i * hb - 1, 0), 0)),
            pl.BlockSpec((halo, cw), lambda i: (jnp.minimum((i + 1) * hb, last_hb), 0)),
            rows(cw),
            rows(o.shape[1]),
            pl.BlockSpec((3, cw), lambda i: (0, 0)),
            resident(wa), resident(wb), resident(wo),
        ],
        out_specs=rows(d),
        out_shape=jax.ShapeDtypeStruct((n, d), jnp.float32),
        compiler_params=_params("parallel"),
        name="merge",
    )(h, gates, gates, z, z, z, b, o, conv_w, wa, wb, wo)


def _ple_kernel(h_ref, g_ref, p_ref, wg_ref, wp_ref, o_ref):
    h = h_ref[...]
    gate = jax.nn.sigmoid(_dot(_rms(h, g_ref[...]).astype(jnp.bfloat16), wg_ref[...]))
    o_ref[...] = h + gate * _dot(p_ref[...].astype(jnp.bfloat16), wp_ref[...])


def _ple(h, g, p, wg, wp):
    n, d = h.shape
    e = p.shape[1]
    tm = ROW_TILE
    return pl.pallas_call(
        _ple_kernel,
        grid=(n // tm,),
        in_specs=[
            pl.BlockSpec((tm, d), lambda i: (i, 0)),
            pl.BlockSpec((1, d), lambda i: (0, 0)),
            pl.BlockSpec((tm, e), lambda i: (i, 0)),
            pl.BlockSpec((d, d), lambda i: (0, 0)),
            pl.BlockSpec((e, d), lambda i: (0, 0)),
        ],
        out_specs=pl.BlockSpec((tm, d), lambda i: (i, 0)),
        out_shape=jax.ShapeDtypeStruct((n, d), jnp.float32),
        compiler_params=_params("parallel"),
        name="ple",
    )(h, g, p, wg, wp)


def _bf16(w):
    return w.astype(jnp.bfloat16)


def kernel(x, p, ffn1_norm, ffn1_w1, ffn1_w3, ffn1_w2, mix_norm, w_in, conv_w, q_norm, k_norm,
           lam_q1, lam_k1, lam_q2, lam_k2, sub_norm, rel_bias, w_branch_a, w_branch_b, w_gate,
           w_out, ffn2_norm, ffn2_w1, ffn2_w3, ffn2_w2, ple_norm, w_ple_gate, w_ple_proj):
    batch, seq, d = x.shape
    n = batch * seq
    depth = ffn1_w1.shape[0]
    h = x.reshape(n, d)

    for l in range(depth):
        row = lambda a: a[l].reshape(1, -1)
        h = _ffn(h, row(ffn1_norm), _bf16(ffn1_w1[l]), _bf16(ffn1_w3[l]), _bf16(ffn1_w2[l]))

        u, z, b_gate, q, k, v = _proj(h, row(mix_norm), _bf16(w_in[l]), row(q_norm), row(k_norm))

        lam_init = 0.8 - 0.6 * math.exp(-0.3 * l)
        lam_rows = jnp.stack([lam_q1[l], lam_k1[l], lam_q2[l], lam_k2[l]])
        table_flat, fixed_flag = _shifted_bias_table(rel_bias, q_norm[l], k_norm[l])
        bias_tiles = _bias_tiles(table_flat, ATT_TILE)
        o = _attn(fixed_flag, lam_rows, q, k, v, bias_tiles, row(sub_norm), batch, seq, lam_init)

        gates = _gates(u, _bf16(w_gate[l]))
        h = _merge(h, gates, z, b_gate, o, conv_w[l],
                   _bf16(w_branch_a[l]), _bf16(w_branch_b[l]), _bf16(w_out[l]), seq)

        h = _ffn(h, row(ffn2_norm), _bf16(ffn2_w1[l]), _bf16(ffn2_w3[l]), _bf16(ffn2_w2[l]))
        h = _ple(h, row(ple_norm), p[l].reshape(n, -1), _bf16(w_ple_gate[l]), _bf16(w_ple_proj[l]))
    return h.reshape(batch, seq, d)
```

```python
import functools
import math

import jax
import jax.numpy as jnp
from jax import lax
from jax.experimental import pallas as pl
from jax.experimental.pallas import tpu as pltpu

EPS = 1e-6
N_BUCKETS = 32
ATT_HEADS = 4
ATT_HEAD_DIM = 128
HEAD_W = 2 * ATT_HEAD_DIM
LANES = 128
NEG_INIT = -1e30
LOG2E = 1.0 / math.log(2.0)
Q_SCALE = LOG2E * ATT_HEAD_DIM ** -0.5

FIXED_SHIFT_MAX_SPAN = 100.0
BF16_ROUNDING_MARGIN = 1.01

VMEM_LIMIT_BYTES = 56 * 1024 * 1024

ROW_TILE = 512
PROJ_ROW_TILE = 1024
FFN_ROW_TILE = 1024
FF_TILE = 512
ATT_TILE = 512
ATT_TILES_PER_STEP = 2
ATT_KEYS_PER_DOT = 256
BIAS_TILE_REACH = 2
MERGE_COL_TILE = 512

_LOG_BUCKET_STARTS = (12, 16, 23, 32, 46, 64, 91)
FAR_DISTANCE = _LOG_BUCKET_STARTS[-1]


def _dot(a, b):
    return jnp.dot(a, b, preferred_element_type=jnp.float32)


def _dot_nt(a, b):
    return lax.dot_general(a, b, (((1,), (1,)), ((), ())),
                           preferred_element_type=jnp.float32)


def _rms(x, g):
    ms = jnp.mean(x * x, axis=-1, keepdims=True)
    return x * lax.rsqrt(ms + EPS) * g


def _early_rows(tile, width, n_tiles, switch_step):
    def index(i, j):
        return (jnp.where(j < switch_step, i, jnp.minimum(i + 1, n_tiles - 1)), 0)
    return pl.BlockSpec((tile, width), index)


def _params(*sem):
    return pltpu.CompilerParams(dimension_semantics=sem,
                                vmem_limit_bytes=VMEM_LIMIT_BYTES)


def _ffn_kernel(x_ref, g_ref, w13_ref, w2_ref, o_ref, xn_ref):
    j = pl.program_id(1)
    tf = w2_ref.shape[0]

    def half_swiglu(xn):
        gate_up = _dot(xn, w13_ref[...])
        gate = gate_up[:, :tf]
        up = gate_up[:, tf:]
        act = (0.5 * gate * jax.nn.sigmoid(gate) * up).astype(jnp.bfloat16)
        return _dot(act, w2_ref[...])

    @pl.when(j == 0)
    def _():
        x = x_ref[...]
        xn = _rms(x, g_ref[...]).astype(jnp.bfloat16)
        xn_ref[...] = xn
        o_ref[...] = x + half_swiglu(xn)

    @pl.when(j != 0)
    def _():
        o_ref[...] += half_swiglu(xn_ref[...])


def _ffn(x, g, w1, w3, w2):
    n, d = x.shape
    f = w1.shape[1]
    tm, tf = FFN_ROW_TILE, FF_TILE
    w13 = jnp.concatenate([w1.reshape(d, f // tf, tf), w3.reshape(d, f // tf, tf)],
                          axis=2).reshape(d, 2 * f)
    return pl.pallas_call(
        _ffn_kernel,
        grid=(n // tm, f // tf),
        in_specs=[
            _early_rows(tm, d, n // tm, f // tf // 2),
            pl.BlockSpec((1, d), lambda i, j: (0, 0)),
            pl.BlockSpec((d, 2 * tf), lambda i, j: (0, j)),
            pl.BlockSpec((tf, d), lambda i, j: (j, 0)),
        ],
        out_specs=pl.BlockSpec((tm, d), lambda i, j: (i, 0)),
        out_shape=jax.ShapeDtypeStruct((n, d), jnp.float32),
        scratch_shapes=[pltpu.VMEM((tm, d), jnp.bfloat16)],
        compiler_params=_params("parallel", "arbitrary"),
        name="ffn",
    )(x, g, w13, w2)


def _proj_kernel(h_ref, g_ref, wa_ref, wc_ref, wb_ref, wq_ref, wk_ref, wv_ref, qg_ref, kg_ref,
                 u_ref, z_ref, b_ref, q_ref, k_ref, v_ref):
    d = ATT_HEAD_DIM

    def project(u):
        z_ref[...] = _dot(u, wc_ref[...]) * _dot(u, wa_ref[...])
        b_ref[...] = _dot(u, wb_ref[...])
        yq = _dot(u, wq_ref[...])
        yk = _dot(u, wk_ref[...])
        for c in range(2):
            cols = slice(c * d, (c + 1) * d)
            q_ref[:, cols] = (_rms(yq[:, cols], qg_ref[...]) * Q_SCALE).astype(jnp.bfloat16)
            k_ref[:, cols] = _rms(yk[:, cols], kg_ref[...]).astype(jnp.bfloat16)
        v_ref[...] = _dot(u, wv_ref[...]).astype(jnp.bfloat16)

    @pl.when(pl.program_id(1) == 0)
    def _():
        u = _rms(h_ref[...], g_ref[...]).astype(jnp.bfloat16)
        u_ref[...] = u
        project(u)

    @pl.when(pl.program_id(1) != 0)
    def _():
        project(u_ref[...])


def _proj(h, g, w_in, q_gain, k_gain):
    n, d = h.shape
    tm, w = PROJ_ROW_TILE, HEAD_W
    nblk = w_in.shape[1] // (6 * w)
    col = pl.BlockSpec((tm, w), lambda i, j: (i, j))
    wspec = lambda p: pl.BlockSpec((d, w), lambda i, j: (0, p * nblk + j))
    gain = pl.BlockSpec((1, ATT_HEAD_DIM), lambda i, j: (0, 0))
    f32 = jax.ShapeDtypeStruct((n, nblk * w), jnp.float32)
    bf16 = jax.ShapeDtypeStruct((n, nblk * w), jnp.bfloat16)
    return pl.pallas_call(
        _proj_kernel,
        grid=(n // tm, nblk),
        in_specs=[
            _early_rows(tm, d, n // tm, nblk // 2),
            pl.BlockSpec((1, d), lambda i, j: (0, 0)),
            wspec(0), wspec(1), wspec(2), wspec(3), wspec(4), wspec(5),
            gain, gain,
        ],
        out_specs=[pl.BlockSpec((tm, d), lambda i, j: (i, 0)), col, col, col, col, col],
        out_shape=[jax.ShapeDtypeStruct((n, d), jnp.bfloat16), f32, f32, bf16, bf16, bf16],
        compiler_params=_params("parallel", "arbitrary"),
        name="proj",
    )(h, g, w_in, w_in, w_in, w_in, w_in, w_in, q_gain, k_gain)


def _bias_tile_kernel(tab_ref, o_ref):
    h = pl.program_id(0)
    t = o_ref.shape[-1]
    offset = (pl.program_id(1) - BIAS_TILE_REACH) * t
    rel = (lax.broadcasted_iota(jnp.int32, (t, t), 1)
           - lax.broadcasted_iota(jnp.int32, (t, t), 0) + offset)
    n = jnp.abs(rel)
    nb = N_BUCKETS // 2
    max_exact = nb // 2
    large = jnp.full((t, t), max_exact, jnp.int32)
    for start in _LOG_BUCKET_STARTS:
        large = large + (n >= start).astype(jnp.int32)
    bucket = jnp.where(rel > 0, nb, 0) + jnp.where(n < max_exact, n, large)
    acc = jnp.zeros((t, t), jnp.float32)
    for bkt in range(N_BUCKETS):
        acc = jnp.where(bucket == bkt, tab_ref[bkt * ATT_HEADS + h], acc)
    o_ref[0, 0] = acc


def _bias_tiles(table_flat, t):
    n_off = 2 * BIAS_TILE_REACH + 1
    return pl.pallas_call(
        _bias_tile_kernel,
        grid=(ATT_HEADS, n_off),
        in_specs=[pl.BlockSpec(memory_space=pltpu.SMEM)],
        out_specs=pl.BlockSpec((1, 1, t, t), lambda h, o: (h, o, 0, 0)),
        out_shape=jax.ShapeDtypeStruct((ATT_HEADS, n_off, t, t), jnp.float32),
        compiler_params=_params("parallel", "parallel"),
        name="bias_tiles",
    )(table_flat)


def _attn_kernel(fixed_ref, lam_ref, q_ref, k_ref, v_ref, bias_ref, sg_ref, o_ref,
                 m_ref, l_ref, acc_ref, *, lam_init):
    t = bias_ref.shape[-1]
    tiles = q_ref.shape[0] // t
    n_chunks = k_ref.shape[0] // t
    d = ATT_HEAD_DIM

    def bias_tile(qt, kc):
        off = jnp.clip(kc - qt, -BIAS_TILE_REACH, BIAS_TILE_REACH)
        return bias_ref[0, off + BIAS_TILE_REACH]

    def finish(q_rows, acc, l):
        lam_q1, lam_k1, lam_q2, lam_k2 = (lam_ref[r:r + 1, :] for r in range(4))
        lam = (jnp.exp(jnp.sum(lam_q1 * lam_k1, axis=-1, keepdims=True))
               - jnp.exp(jnp.sum(lam_q2 * lam_k2, axis=-1, keepdims=True)) + lam_init)
        o = acc[0] / l[0] - lam * (acc[1] / l[1])
        o_ref[q_rows, :] = (_rms(o, sg_ref[...]) * (1.0 - lam_init)).astype(jnp.bfloat16)

    def fixed_shift_tile(sub):
        qt = pl.program_id(2) * tiles + sub
        q_rows = slice(sub * t, (sub + 1) * t)
        acc = [jnp.zeros((t, HEAD_W), jnp.float32) for _ in range(2)]
        lp = [jnp.zeros((t, LANES), jnp.float32) for _ in range(2)]
        kd = ATT_KEYS_PER_DOT
        for kc in range(n_chunks):
            tile = bias_tile(qt, kc)
            for part in range(t // kd):
                rows = slice(kc * t + part * kd, kc * t + (part + 1) * kd)
                bias = tile[:, part * kd:(part + 1) * kd]
                v = v_ref[rows, :]
                for c in range(2):
                    cols = slice(c * d, (c + 1) * d)
                    p = jnp.exp2(_dot_nt(q_ref[q_rows, cols], k_ref[rows, cols]) + bias)
                    for s in range(kd // LANES):
                        lp[c] = lp[c] + p[:, s * LANES:(s + 1) * LANES]
                    acc[c] = acc[c] + _dot(p.astype(jnp.bfloat16), v)
        finish(q_rows, acc, [jnp.sum(x, axis=-1, keepdims=True) for x in lp])

    def fixed_shift_block():
        for sub in range(tiles):
            fixed_shift_tile(sub)

    def online_step(sub, qt, kc):
        rows = pl.ds(pl.multiple_of(kc * t, t), t)
        bias = bias_tile(qt, kc)
        v = v_ref[rows, :]
        for c in range(2):
            cols = slice(c * d, (c + 1) * d)
            x = _dot_nt(q_ref[sub * t:(sub + 1) * t, cols], k_ref[rows, cols]) + bias
            m_old = m_ref[c]
            m_new = jnp.maximum(m_old, jnp.max(x, axis=-1, keepdims=True))
            alpha = jnp.exp2(m_old - m_new)
            p = jnp.exp2(x - m_new)
            l_ref[c] = alpha * l_ref[c] + jnp.sum(p, axis=-1, keepdims=True)
            acc_ref[c] = alpha * acc_ref[c] + _dot(p.astype(jnp.bfloat16), v)
            m_ref[c] = m_new

    def online_block():
        for sub in range(tiles):
            qt = pl.program_id(2) * tiles + sub
            m_ref[...] = jnp.full_like(m_ref, NEG_INIT)
            l_ref[...] = jnp.zeros_like(l_ref)
            acc_ref[...] = jnp.zeros_like(acc_ref)

            def body(kc, carry, sub=sub, qt=qt):
                online_step(sub, qt, kc)
                return carry

            lax.fori_loop(0, n_chunks, body, 0)
            finish(slice(sub * t, (sub + 1) * t), [acc_ref[0], acc_ref[1]], [l_ref[0], l_ref[1]])

    pl.when(fixed_ref[0] == 1)(fixed_shift_block)
    pl.when(fixed_ref[0] != 1)(online_block)


def _attn(fixed_flag, lam_rows, q, k, v, bias_tiles, sub_gain, batch, seq, lam_init):
    n, width = q.shape
    t = ATT_TILE
    rows = ATT_TILES_PER_STEP * t
    assert FAR_DISTANCE <= (BIAS_TILE_REACH - 1) * t + 1 and seq % rows == 0
    nq = seq // rows
    w = HEAD_W
    return pl.pallas_call(
        functools.partial(_attn_kernel, lam_init=lam_init),
        grid=(batch, ATT_HEADS, nq),
        in_specs=[
            pl.BlockSpec(memory_space=pltpu.SMEM),
            pl.BlockSpec((4, ATT_HEAD_DIM), lambda b, h, i: (0, 0)),
            pl.BlockSpec((rows, w), lambda b, h, i: (b * nq + i, h)),
            pl.BlockSpec((seq, w), lambda b, h, i: (b, h)),
            pl.BlockSpec((seq, w), lambda b, h, i: (b, h)),
            pl.BlockSpec((1, 2 * BIAS_TILE_REACH + 1, t, t), lambda b, h, i: (h, 0, 0, 0)),
            pl.BlockSpec((1, w), lambda b, h, i: (0, 0)),
        ],
        out_specs=pl.BlockSpec((rows, w), lambda b, h, i: (b * nq + i, h)),
        out_shape=jax.ShapeDtypeStruct((n, width), jnp.bfloat16),
        scratch_shapes=[
            pltpu.VMEM((2, t, 1), jnp.float32),
            pltpu.VMEM((2, t, 1), jnp.float32),
            pltpu.VMEM((2, t, w), jnp.float32),
        ],
        compiler_params=_params("parallel", "parallel", "arbitrary"),
        name="attn",
    )(fixed_flag, lam_rows, q, k, v, bias_tiles, sub_gain)


def _shifted_bias_table(rel_bias, q_gain, k_gain):
    table = rel_bias.astype(jnp.float32) * LOG2E
    dot_bound = (Q_SCALE * ATT_HEAD_DIM * BF16_ROUNDING_MARGIN
                 * jnp.max(jnp.abs(q_gain)) * jnp.max(jnp.abs(k_gain)))
    span = 2.0 * dot_bound + (jnp.max(table) - jnp.min(table))
    fixed = (span <= FIXED_SHIFT_MAX_SPAN).astype(jnp.int32).reshape(1)
    shift = jnp.where(fixed[0] == 1, dot_bound + jnp.max(table), 0.0)
    return (table - shift).reshape(-1), fixed


def _merge_kernel(h_ref, u_ref, z_ref, zp_ref, zn_ref, b_ref, o_ref, cw_ref,
                  wga_ref, wgb_ref, wa_ref, wb_ref, wo_ref, out_ref, a_ref, *, tiles_per_seq):
    i = pl.program_id(0)
    j = pl.program_id(1)
    tm = h_ref.shape[0]
    halo = zp_ref.shape[0]

    def mix(a):
        u = u_ref[...]
        g_a = jax.nn.sigmoid(_dot(u, wga_ref[...]))
        g_b = jax.nn.sigmoid(_dot(u, wgb_ref[...]))
        y_a = _dot(a, wa_ref[...])
        y_b = _dot(o_ref[...], wb_ref[...])
        return _dot((g_a * y_a + g_b * y_b).astype(jnp.bfloat16), wo_ref[...])

    @pl.when(j == 0)
    def _():
        z = z_ref[...]
        pos = i % tiles_per_seq
        before = jnp.where(pos == 0, 0.0, zp_ref[halo - 1:halo, :])
        after = jnp.where(pos == tiles_per_seq - 1, 0.0, zn_ref[0:1, :])
        row = lax.broadcasted_iota(jnp.int32, z.shape, 0)
        z_prev = jnp.where(row == 0, before, pltpu.roll(z, 1, 0))
        z_next = jnp.where(row == tm - 1, after, pltpu.roll(z, tm - 1, 0))
        conv = z_prev * cw_ref[0:1, :] + z * cw_ref[1:2, :] + z_next * cw_ref[2:3, :]
        a = (b_ref[...] * conv).astype(jnp.bfloat16)
        a_ref[...] = a
        out_ref[...] = h_ref[...] + mix(a)

    @pl.when(j != 0)
    def _():
        out_ref[...] += mix(a_ref[...])


def _merge(h, u, z, b, o, conv_w, wg, wa, wb, wo, seq):
    n, d = h.shape
    cw = z.shape[1]
    tm, tn = ROW_TILE, MERGE_COL_TILE
    halo = 8
    nj = d // tn
    hb = tm // halo
    last_hb = n // halo - 1
    n_tiles = n // tm
    rows = lambda width: pl.BlockSpec((tm, width), lambda i, j: (i, 0))
    z_tile = lambda i, j: jnp.where(j < 1, i, jnp.minimum(i + 1, n_tiles - 1))
    return pl.pallas_call(
        functools.partial(_merge_kernel, tiles_per_seq=seq // tm),
        grid=(n // tm, nj),
        in_specs=[
            _early_rows(tm, d, n_tiles, 3),
            rows(d),
            _early_rows(tm, cw, n_tiles, 1),
            pl.BlockSpec((halo, cw), lambda i, j: (jnp.maximum(z_tile(i, j) * hb - 1, 0), 0)),
            pl.BlockSpec((halo, cw), lambda i, j: (jnp.minimum((z_tile(i, j) + 1) * hb, last_hb), 0)),
            _early_rows(tm, cw, n_tiles, 2),
            rows(o.shape[1]),
            pl.BlockSpec((3, cw), lambda i, j: (0, 0)),
            pl.BlockSpec((d, tn), lambda i, j: (0, j)),
            pl.BlockSpec((d, tn), lambda i, j: (0, nj + j)),
            pl.BlockSpec((cw, tn), lambda i, j: (0, j)),
            pl.BlockSpec((o.shape[1], tn), lambda i, j: (0, j)),
            pl.BlockSpec((tn, d), lambda i, j: (j, 0)),
        ],
        out_specs=rows(d),
        out_shape=jax.ShapeDtypeStruct((n, d), jnp.float32),
        scratch_shapes=[pltpu.VMEM((tm, cw), jnp.bfloat16)],
        compiler_params=_params("parallel", "arbitrary"),
        name="merge",
    )(h, u, z, z, z, b, o, conv_w, wg, wg, wa, wb, wo)


def _ple_kernel(h_ref, g_ref, p_ref, wg_ref, wp_ref, o_ref):
    h = h_ref[...]
    gate = jax.nn.sigmoid(_dot(_rms(h, g_ref[...]).astype(jnp.bfloat16), wg_ref[...]))
    o_ref[...] = h + gate * _dot(p_ref[...].astype(jnp.bfloat16), wp_ref[...])


def _ple(h, g, p, wg, wp):
    n, d = h.shape
    e = p.shape[1]
    tm = ROW_TILE
    return pl.pallas_call(
        _ple_kernel,
        grid=(n // tm,),
        in_specs=[
            pl.BlockSpec((tm, d), lambda i: (i, 0)),
            pl.BlockSpec((1, d), lambda i: (0, 0)),
            pl.BlockSpec((tm, e), lambda i: (i, 0)),
            pl.BlockSpec((d, d), lambda i: (0, 0)),
            pl.BlockSpec((e, d), lambda i: (0, 0)),
        ],
        out_specs=pl.BlockSpec((tm, d), lambda i: (i, 0)),
        out_shape=jax.ShapeDtypeStruct((n, d), jnp.float32),
        compiler_params=_params("parallel"),
        name="ple",
    )(h, g, p, wg, wp)


def _bf16(w):
    return w.astype(jnp.bfloat16)


def kernel(x, p, ffn1_norm, ffn1_w1, ffn1_w3, ffn1_w2, mix_norm, w_in, conv_w, q_norm, k_norm,
           lam_q1, lam_k1, lam_q2, lam_k2, sub_norm, rel_bias, w_branch_a, w_branch_b, w_gate,
           w_out, ffn2_norm, ffn2_w1, ffn2_w3, ffn2_w2, ple_norm, w_ple_gate, w_ple_proj):
    batch, seq, d = x.shape
    n = batch * seq
    depth = ffn1_w1.shape[0]
    h = x.reshape(n, d)

    for l in range(depth):
        row = lambda a: a[l].reshape(1, -1)
        h = _ffn(h, row(ffn1_norm), _bf16(ffn1_w1[l]), _bf16(ffn1_w3[l]), _bf16(ffn1_w2[l]))

        u, z, b_gate, q, k, v = _proj(h, row(mix_norm), _bf16(w_in[l]), row(q_norm), row(k_norm))

        lam_init = 0.8 - 0.6 * math.exp(-0.3 * l)
        lam_rows = jnp.stack([lam_q1[l], lam_k1[l], lam_q2[l], lam_k2[l]])
        table_flat, fixed_flag = _shifted_bias_table(rel_bias, q_norm[l], k_norm[l])
        bias_tiles = _bias_tiles(table_flat, ATT_TILE)
        o = _attn(fixed_flag, lam_rows, q, k, v, bias_tiles, row(sub_norm), batch, seq, lam_init)

        h = _merge(h, u, z, b_gate, o, conv_w[l], _bf16(w_gate[l]),
                   _bf16(w_branch_a[l]), _bf16(w_branch_b[l]), _bf16(w_out[l]), seq)

        h = _ffn(h, row(ffn2_norm), _bf16(ffn2_w1[l]), _bf16(ffn2_w3[l]), _bf16(ffn2_w2[l]))
        h = _ple(h, row(ple_norm), p[l].reshape(n, -1), _bf16(w_ple_gate[l]), _bf16(w_ple_proj[l]))
    return h.reshape(batch, seq, d)
```

```python
import functools
import math

import jax
import jax.numpy as jnp
from jax import lax
from jax.experimental import pallas as pl
from jax.experimental.pallas import tpu as pltpu

EPS = 1e-6
N_BUCKETS = 32
ATT_HEADS = 4
ATT_HEAD_DIM = 128
HEAD_W = 2 * ATT_HEAD_DIM
LANES = 128
NEG_INIT = -1e30
LOG2E = 1.0 / math.log(2.0)
Q_SCALE = LOG2E * ATT_HEAD_DIM ** -0.5

FIXED_SHIFT_MAX_SPAN = 100.0
BF16_ROUNDING_MARGIN = 1.01

VMEM_LIMIT_BYTES = 56 * 1024 * 1024

ROW_TILE = 512
PROJ_ROW_TILE = 1024
FFN_ROW_TILE = 1024
FF_TILE = 512
ATT_TILE = 512
ATT_TILES_PER_STEP = 2
ATT_KEYS_PER_DOT = 256
BIAS_TILE_REACH = 2
MERGE_COL_TILE = 512

_LOG_BUCKET_STARTS = (12, 16, 23, 32, 46, 64, 91)
FAR_DISTANCE = _LOG_BUCKET_STARTS[-1]


def _dot(a, b):
    return jnp.dot(a, b, preferred_element_type=jnp.float32)


def _dot_nt(a, b):
    return lax.dot_general(a, b, (((1,), (1,)), ((), ())),
                           preferred_element_type=jnp.float32)


def _rms(x, g):
    ms = jnp.mean(x * x, axis=-1, keepdims=True)
    return x * lax.rsqrt(ms + EPS) * g


def _early_rows(tile, width, n_tiles, switch_step):
    def index(i, j):
        return (jnp.where(j < switch_step, i, jnp.minimum(i + 1, n_tiles - 1)), 0)
    return pl.BlockSpec((tile, width), index)


def _params(*sem):
    return pltpu.CompilerParams(dimension_semantics=sem,
                                vmem_limit_bytes=VMEM_LIMIT_BYTES)


def _ffn_kernel(x_ref, g_ref, w1_ref, w3_ref, w2_ref, o_ref, xn_ref):
    j = pl.program_id(1)

    def half_swiglu(xn):
        gate = _dot(xn, w1_ref[...])
        up = _dot(xn, w3_ref[...])
        act = (0.5 * gate * jax.nn.sigmoid(gate) * up).astype(jnp.bfloat16)
        return _dot(act, w2_ref[...])

    @pl.when(j == 0)
    def _():
        x = x_ref[...]
        xn = _rms(x, g_ref[...]).astype(jnp.bfloat16)
        xn_ref[...] = xn
        o_ref[...] = x + half_swiglu(xn)

    @pl.when(j != 0)
    def _():
        o_ref[...] += half_swiglu(xn_ref[...])


def _ffn(x, g, w1, w3, w2):
    n, d = x.shape
    f = w1.shape[1]
    tm, tf = FFN_ROW_TILE, FF_TILE
    return pl.pallas_call(
        _ffn_kernel,
        grid=(n // tm, f // tf),
        in_specs=[
            _early_rows(tm, d, n // tm, f // tf // 2),
            pl.BlockSpec((1, d), lambda i, j: (0, 0)),
            pl.BlockSpec((d, tf), lambda i, j: (0, j)),
            pl.BlockSpec((d, tf), lambda i, j: (0, j)),
            pl.BlockSpec((tf, d), lambda i, j: (j, 0)),
        ],
        out_specs=pl.BlockSpec((tm, d), lambda i, j: (i, 0)),
        out_shape=jax.ShapeDtypeStruct((n, d), jnp.float32),
        scratch_shapes=[pltpu.VMEM((tm, d), jnp.bfloat16)],
        compiler_params=_params("parallel", "arbitrary"),
        name="ffn",
    )(x, g, w1, w3, w2)


def _proj_kernel(h_ref, g_ref, wa_ref, wc_ref, wb_ref, wq_ref, wk_ref, wv_ref, qg_ref, kg_ref,
                 u_ref, z_ref, b_ref, q_ref, k_ref, v_ref):
    d = ATT_HEAD_DIM

    def project(u):
        z_ref[...] = _dot(u, wc_ref[...]) * _dot(u, wa_ref[...])
        b_ref[...] = _dot(u, wb_ref[...])
        yq = _dot(u, wq_ref[...])
        yk = _dot(u, wk_ref[...])
        for c in range(2):
            cols = slice(c * d, (c + 1) * d)
            q_ref[:, cols] = (_rms(yq[:, cols], qg_ref[...]) * Q_SCALE).astype(jnp.bfloat16)
            k_ref[:, cols] = _rms(yk[:, cols], kg_ref[...]).astype(jnp.bfloat16)
        v_ref[...] = _dot(u, wv_ref[...]).astype(jnp.bfloat16)

    @pl.when(pl.program_id(1) == 0)
    def _():
        u = _rms(h_ref[...], g_ref[...]).astype(jnp.bfloat16)
        u_ref[...] = u
        project(u)

    @pl.when(pl.program_id(1) != 0)
    def _():
        project(u_ref[...])


def _proj(h, g, w_in, q_gain, k_gain):
    n, d = h.shape
    tm, w = PROJ_ROW_TILE, HEAD_W
    nblk = w_in.shape[1] // (6 * w)
    col = pl.BlockSpec((tm, w), lambda i, j: (i, j))
    wspec = lambda p: pl.BlockSpec((d, w), lambda i, j: (0, p * nblk + j))
    gain = pl.BlockSpec((1, ATT_HEAD_DIM), lambda i, j: (0, 0))
    f32 = jax.ShapeDtypeStruct((n, nblk * w), jnp.float32)
    bf16 = jax.ShapeDtypeStruct((n, nblk * w), jnp.bfloat16)
    return pl.pallas_call(
        _proj_kernel,
        grid=(n // tm, nblk),
        in_specs=[
            _early_rows(tm, d, n // tm, nblk // 2),
            pl.BlockSpec((1, d), lambda i, j: (0, 0)),
            wspec(0), wspec(1), wspec(2), wspec(3), wspec(4), wspec(5),
            gain, gain,
        ],
        out_specs=[pl.BlockSpec((tm, d), lambda i, j: (i, 0)), col, col, col, col, col],
        out_shape=[jax.ShapeDtypeStruct((n, d), jnp.bfloat16), f32, f32, bf16, bf16, bf16],
        compiler_params=_params("parallel", "arbitrary"),
        name="proj",
    )(h, g, w_in, w_in, w_in, w_in, w_in, w_in, q_gain, k_gain)


def _bias_tile_kernel(tab_ref, o_ref):
    h = pl.program_id(0)
    t = o_ref.shape[-1]
    chunk_off = pl.program_id(1) - BIAS_TILE_REACH
    nb = N_BUCKETS // 2
    max_exact = nb // 2

    @pl.when(chunk_off == -BIAS_TILE_REACH)
    def _():
        o_ref[0, 0] = jnp.full((t, t), tab_ref[(nb - 1) * ATT_HEADS + h], jnp.float32)

    @pl.when(chunk_off == BIAS_TILE_REACH)
    def _():
        o_ref[0, 0] = jnp.full((t, t), tab_ref[(N_BUCKETS - 1) * ATT_HEADS + h], jnp.float32)

    @pl.when(jnp.abs(chunk_off) < BIAS_TILE_REACH)
    def _():
        rel = (lax.broadcasted_iota(jnp.int32, (t, t), 1)
               - lax.broadcasted_iota(jnp.int32, (t, t), 0) + chunk_off * t)
        n = jnp.abs(rel)
        large = jnp.full((t, t), max_exact, jnp.int32)
        for start in _LOG_BUCKET_STARTS:
            large = large + (n >= start).astype(jnp.int32)
        bucket = jnp.where(rel > 0, nb, 0) + jnp.where(n < max_exact, n, large)
        acc = jnp.zeros((t, t), jnp.float32)
        for bkt in range(N_BUCKETS):
            acc = jnp.where(bucket == bkt, tab_ref[bkt * ATT_HEADS + h], acc)
        o_ref[0, 0] = acc


def _bias_tiles(table_flat, t):
    n_off = 2 * BIAS_TILE_REACH + 1
    return pl.pallas_call(
        _bias_tile_kernel,
        grid=(ATT_HEADS, n_off),
        in_specs=[pl.BlockSpec(memory_space=pltpu.SMEM)],
        out_specs=pl.BlockSpec((1, 1, t, t), lambda h, o: (h, o, 0, 0)),
        out_shape=jax.ShapeDtypeStruct((ATT_HEADS, n_off, t, t), jnp.float32),
        compiler_params=_params("parallel", "parallel"),
        name="bias_tiles",
    )(table_flat)


def _attn_kernel(fixed_ref, lam_ref, q_ref, k_ref, v_ref, bias_ref, sg_ref, o_ref,
                 m_ref, l_ref, acc_ref, *, lam_init):
    t = bias_ref.shape[-1]
    tiles = q_ref.shape[0] // t
    n_chunks = k_ref.shape[0] // t
    d = ATT_HEAD_DIM

    def bias_tile(qt, kc):
        off = jnp.clip(kc - qt, -BIAS_TILE_REACH, BIAS_TILE_REACH)
        return bias_ref[0, off + BIAS_TILE_REACH]

    def finish(q_rows, acc, l):
        lam_q1, lam_k1, lam_q2, lam_k2 = (lam_ref[r:r + 1, :] for r in range(4))
        lam = (jnp.exp(jnp.sum(lam_q1 * lam_k1, axis=-1, keepdims=True))
               - jnp.exp(jnp.sum(lam_q2 * lam_k2, axis=-1, keepdims=True)) + lam_init)
        o = acc[0] / l[0] - lam * (acc[1] / l[1])
        o_ref[q_rows, :] = (_rms(o, sg_ref[...]) * (1.0 - lam_init)).astype(jnp.bfloat16)

    def fixed_shift_tile(sub):
        qt = pl.program_id(2) * tiles + sub
        q_rows = slice(sub * t, (sub + 1) * t)
        acc = [jnp.zeros((t, HEAD_W), jnp.float32) for _ in range(2)]
        lp = [jnp.zeros((t, LANES), jnp.float32) for _ in range(2)]
        kd = ATT_KEYS_PER_DOT
        for kc in range(n_chunks):
            tile = bias_tile(qt, kc)
            for part in range(t // kd):
                rows = slice(kc * t + part * kd, kc * t + (part + 1) * kd)
                bias = tile[:, part * kd:(part + 1) * kd]
                v = v_ref[rows, :]
                for c in range(2):
                    cols = slice(c * d, (c + 1) * d)
                    p = jnp.exp2(_dot_nt(q_ref[q_rows, cols], k_ref[rows, cols]) + bias)
                    for s in range(kd // LANES):
                        lp[c] = lp[c] + p[:, s * LANES:(s + 1) * LANES]
                    acc[c] = acc[c] + _dot(p.astype(jnp.bfloat16), v)
        finish(q_rows, acc, [jnp.sum(x, axis=-1, keepdims=True) for x in lp])

    def fixed_shift_block():
        for sub in range(tiles):
            fixed_shift_tile(sub)

    def online_step(sub, qt, kc):
        rows = pl.ds(pl.multiple_of(kc * t, t), t)
        bias = bias_tile(qt, kc)
        v = v_ref[rows, :]
        for c in range(2):
            cols = slice(c * d, (c + 1) * d)
            x = _dot_nt(q_ref[sub * t:(sub + 1) * t, cols], k_ref[rows, cols]) + bias
            m_old = m_ref[c]
            m_new = jnp.maximum(m_old, jnp.max(x, axis=-1, keepdims=True))
            alpha = jnp.exp2(m_old - m_new)
            p = jnp.exp2(x - m_new)
            l_ref[c] = alpha * l_ref[c] + jnp.sum(p, axis=-1, keepdims=True)
            acc_ref[c] = alpha * acc_ref[c] + _dot(p.astype(jnp.bfloat16), v)
            m_ref[c] = m_new

    def online_block():
        for sub in range(tiles):
            qt = pl.program_id(2) * tiles + sub
            m_ref[...] = jnp.full_like(m_ref, NEG_INIT)
            l_ref[...] = jnp.zeros_like(l_ref)
            acc_ref[...] = jnp.zeros_like(acc_ref)

            def body(kc, carry, sub=sub, qt=qt):
                online_step(sub, qt, kc)
                return carry

            lax.fori_loop(0, n_chunks, body, 0)
            finish(slice(sub * t, (sub + 1) * t), [acc_ref[0], acc_ref[1]], [l_ref[0], l_ref[1]])

    pl.when(fixed_ref[0] == 1)(fixed_shift_block)
    pl.when(fixed_ref[0] != 1)(online_block)


def _attn(fixed_flag, lam_rows, q, k, v, bias_tiles, sub_gain, batch, seq, lam_init):
    n, width = q.shape
    t = ATT_TILE
    rows = ATT_TILES_PER_STEP * t
    assert FAR_DISTANCE <= (BIAS_TILE_REACH - 1) * t + 1 and seq % rows == 0
    nq = seq // rows
    w = HEAD_W
    return pl.pallas_call(
        functools.partial(_attn_kernel, lam_init=lam_init),
        grid=(batch, ATT_HEADS, nq),
        in_specs=[
            pl.BlockSpec(memory_space=pltpu.SMEM),
            pl.BlockSpec((4, ATT_HEAD_DIM), lambda b, h, i: (0, 0)),
            pl.BlockSpec((rows, w), lambda b, h, i: (b * nq + i, h)),
            pl.BlockSpec((seq, w), lambda b, h, i: (b, h)),
            pl.BlockSpec((seq, w), lambda b, h, i: (b, h)),
            pl.BlockSpec((1, 2 * BIAS_TILE_REACH + 1, t, t), lambda b, h, i: (h, 0, 0, 0)),
            pl.BlockSpec((1, w), lambda b, h, i: (0, 0)),
        ],
        out_specs=pl.BlockSpec((rows, w), lambda b, h, i: (b * nq + i, h)),
        out_shape=jax.ShapeDtypeStruct((n, width), jnp.bfloat16),
        scratch_shapes=[
            pltpu.VMEM((2, t, 1), jnp.float32),
            pltpu.VMEM((2, t, 1), jnp.float32),
            pltpu.VMEM((2, t, w), jnp.float32),
        ],
        compiler_params=_params("parallel", "parallel", "arbitrary"),
        name="attn",
    )(fixed_flag, lam_rows, q, k, v, bias_tiles, sub_gain)


def _shifted_bias_table(rel_bias, q_gain, k_gain):
    table = rel_bias.astype(jnp.float32) * LOG2E
    dot_bound = (Q_SCALE * ATT_HEAD_DIM * BF16_ROUNDING_MARGIN
                 * jnp.max(jnp.abs(q_gain)) * jnp.max(jnp.abs(k_gain)))
    span = 2.0 * dot_bound + (jnp.max(table) - jnp.min(table))
    fixed = (span <= FIXED_SHIFT_MAX_SPAN).astype(jnp.int32).reshape(1)
    shift = jnp.where(fixed[0] == 1, dot_bound + jnp.max(table), 0.0)
    return (table - shift).reshape(-1), fixed


def _merge_kernel(h_ref, u_ref, z_ref, zp_ref, zn_ref, b_ref, o_ref, cw_ref,
                  wga_ref, wgb_ref, wa_ref, wb_ref, wo_ref, out_ref, a_ref, *, tiles_per_seq):
    i = pl.program_id(0)
    j = pl.program_id(1)
    tm = h_ref.shape[0]
    halo = zp_ref.shape[0]

    def mix(a):
        u = u_ref[...]
        g_a = jax.nn.sigmoid(_dot(u, wga_ref[...]))
        g_b = jax.nn.sigmoid(_dot(u, wgb_ref[...]))
        y_a = _dot(a, wa_ref[...])
        y_b = _dot(o_ref[...], wb_ref[...])
        return _dot((g_a * y_a + g_b * y_b).astype(jnp.bfloat16), wo_ref[...])

    @pl.when(j == 0)
    def _():
        z = z_ref[...]
        pos = i % tiles_per_seq
        before = jnp.where(pos == 0, 0.0, zp_ref[halo - 1:halo, :])
        after = jnp.where(pos == tiles_per_seq - 1, 0.0, zn_ref[0:1, :])
        row = lax.broadcasted_iota(jnp.int32, z.shape, 0)
        z_prev = jnp.where(row == 0, before, pltpu.roll(z, 1, 0))
        z_next = jnp.where(row == tm - 1, after, pltpu.roll(z, tm - 1, 0))
        conv = z_prev * cw_ref[0:1, :] + z * cw_ref[1:2, :] + z_next * cw_ref[2:3, :]
        a = (b_ref[...] * conv).astype(jnp.bfloat16)
        a_ref[...] = a
        out_ref[...] = h_ref[...] + mix(a)

    @pl.when(j != 0)
    def _():
        out_ref[...] += mix(a_ref[...])


def _merge(h, u, z, b, o, conv_w, wg, wa, wb, wo, seq):
    n, d = h.shape
    cw = z.shape[1]
    tm, tn = ROW_TILE, MERGE_COL_TILE
    halo = 8
    nj = d // tn
    hb = tm // halo
    last_hb = n // halo - 1
    n_tiles = n // tm
    rows = lambda width: pl.BlockSpec((tm, width), lambda i, j: (i, 0))
    z_tile = lambda i, j: jnp.where(j < 1, i, jnp.minimum(i + 1, n_tiles - 1))
    return pl.pallas_call(
        functools.partial(_merge_kernel, tiles_per_seq=seq // tm),
        grid=(n // tm, nj),
        in_specs=[
            _early_rows(tm, d, n_tiles, 3),
            rows(d),
            _early_rows(tm, cw, n_tiles, 1),
            pl.BlockSpec((halo, cw), lambda i, j: (jnp.maximum(z_tile(i, j) * hb - 1, 0), 0)),
            pl.BlockSpec((halo, cw), lambda i, j: (jnp.minimum((z_tile(i, j) + 1) * hb, last_hb), 0)),
            _early_rows(tm, cw, n_tiles, 2),
            rows(o.shape[1]),
            pl.BlockSpec((3, cw), lambda i, j: (0, 0)),
            pl.BlockSpec((d, tn), lambda i, j: (0, j)),
            pl.BlockSpec((d, tn), lambda i, j: (0, nj + j)),
            pl.BlockSpec((cw, tn), lambda i, j: (0, j)),
            pl.BlockSpec((o.shape[1], tn), lambda i, j: (0, j)),
            pl.BlockSpec((tn, d), lambda i, j: (j, 0)),
        ],
        out_specs=rows(d),
        out_shape=jax.ShapeDtypeStruct((n, d), jnp.float32),
        scratch_shapes=[pltpu.VMEM((tm, cw), jnp.bfloat16)],
        compiler_params=_params("parallel", "arbitrary"),
        name="merge",
    )(h, u, z, z, z, b, o, conv_w, wg, wg, wa, wb, wo)


def _ple_kernel(h_ref, g_ref, p_ref, wg_ref, wp_ref, o_ref):
    half = h_ref.shape[0] // 2
    for s in range(2):
        rows = slice(s * half, (s + 1) * half)
        h = h_ref[rows, :]
        gate = jax.nn.sigmoid(_dot(_rms(h, g_ref[...]).astype(jnp.bfloat16), wg_ref[...]))
        o_ref[rows, :] = h + gate * _dot(p_ref[rows, :].astype(jnp.bfloat16), wp_ref[...])


def _ple(h, g, p, wg, wp):
    n, d = h.shape
    e = p.shape[1]
    tm = ROW_TILE
    return pl.pallas_call(
        _ple_kernel,
        grid=(n // tm,),
        in_specs=[
            pl.BlockSpec((tm, d), lambda i: (i, 0)),
            pl.BlockSpec((1, d), lambda i: (0, 0)),
            pl.BlockSpec((tm, e), lambda i: (i, 0)),
            pl.BlockSpec((d, d), lambda i: (0, 0)),
            pl.BlockSpec((e, d), lambda i: (0, 0)),
        ],
        out_specs=pl.BlockSpec((tm, d), lambda i: (i, 0)),
        out_shape=jax.ShapeDtypeStruct((n, d), jnp.float32),
        compiler_params=_params("parallel"),
        name="ple",
    )(h, g, p, wg, wp)


def _bf16(w):
    return w.astype(jnp.bfloat16)


def kernel(x, p, ffn1_norm, ffn1_w1, ffn1_w3, ffn1_w2, mix_norm, w_in, conv_w, q_norm, k_norm,
           lam_q1, lam_k1, lam_q2, lam_k2, sub_norm, rel_bias, w_branch_a, w_branch_b, w_gate,
           w_out, ffn2_norm, ffn2_w1, ffn2_w3, ffn2_w2, ple_norm, w_ple_gate, w_ple_proj):
    batch, seq, d = x.shape
    n = batch * seq
    depth = ffn1_w1.shape[0]
    h = x.reshape(n, d)

    for l in range(depth):
        row = lambda a: a[l].reshape(1, -1)
        h = _ffn(h, row(ffn1_norm), _bf16(ffn1_w1[l]), _bf16(ffn1_w3[l]), _bf16(ffn1_w2[l]))

        u, z, b_gate, q, k, v = _proj(h, row(mix_norm), _bf16(w_in[l]), row(q_norm), row(k_norm))

        lam_init = 0.8 - 0.6 * math.exp(-0.3 * l)
        lam_rows = jnp.stack([lam_q1[l], lam_k1[l], lam_q2[l], lam_k2[l]])
        table_flat, fixed_flag = _shifted_bias_table(rel_bias, q_norm[l], k_norm[l])
        bias_tiles = _bias_tiles(table_flat, ATT_TILE)
        o = _attn(fixed_flag, lam_rows, q, k, v, bias_tiles, row(sub_norm), batch, seq, lam_init)

        h = _merge(h, u, z, b_gate, o, conv_w[l], _bf16(w_gate[l]),
                   _bf16(w_branch_a[l]), _bf16(w_branch_b[l]), _bf16(w_out[l]), seq)

        h = _ffn(h, row(ffn2_norm), _bf16(ffn2_w1[l]), _bf16(ffn2_w3[l]), _bf16(ffn2_w2[l]))
        h = _ple(h, row(ple_norm), p[l].reshape(n, -1), _bf16(w_ple_gate[l]), _bf16(w_ple_proj[l]))
    return h.reshape(batch, seq, d)
```
